```python
import jax, jax.numpy as jnp
from jax import lax
import numpy as np

D_MODEL = 1024
BATCH = 2
SEQ = 8192
DEPTH = 2

D_FF = 2816
N_MOD = 9
SSD_INNER = 1024
SSD_HEADDIM = 64
SSD_HEADS = SSD_INNER // SSD_HEADDIM
SSD_GROUPS = 2
SSD_STATE = 128
SSD_CONV = 5
SSD_CHUNK = 128
CONV_CH = SSD_INNER + 2 * SSD_GROUPS * SSD_STATE
MLA_HEADS = 8
QK_NOPE = 128
QK_ROPE = 64
V_DIM = 128
Q_LORA = 512
KV_LORA = 256
ATTN_WIDTH = MLA_HEADS * V_DIM
Q_BLOCK = 128
ROPE_THETA = 10000.0
IN_SPLITS = (SSD_INNER, CONV_CH, 2 * SSD_HEADS, Q_LORA, KV_LORA + QK_ROPE)
IN_COLS = sum(IN_SPLITS)
MIX_WIDTH = SSD_INNER + ATTN_WIDTH
EPS = 1e-6

kernel_name = 'hybrid_ssd_mla_macaron_encoder'


def rmsnorm(x, g):
    xf = x.astype(jnp.float32)
    y = xf * lax.rsqrt(jnp.mean(xf * xf, axis=-1, keepdims=True) + EPS)
    return (y * g.astype(jnp.float32)).astype(x.dtype)


def modulate(x, g, shift, scale):
    return rmsnorm(x, g) * (1.0 + scale[:, None, :]) + shift[:, None, :]


def swiglu(h, w_gate, w_up, w_down):
    return (jax.nn.silu(h @ w_gate) * (h @ w_up)) @ w_down


def dwconv_centred(u, w, b):
    k, ch = w.shape
    out = lax.conv_general_dilated(
        u, w.astype(u.dtype)[:, None, :], window_strides=(1,),
        padding=[(k // 2, k // 2)], dimension_numbers=('NWC', 'WIO', 'NWC'),
        feature_group_count=ch)
    return out + b


def rope(t, cos, sin):
    half = t.shape[-1] // 2
    t1, t2 = t[..., :half], t[..., half:]
    return jnp.concatenate([t1 * cos - t2 * sin, t2 * cos + t1 * sin], axis=-1).astype(t.dtype)


def ssd_chunked(xh, dt, a, bm, cm):
    b, s, h, p = xh.shape
    g, n = bm.shape[2], bm.shape[3]
    e = h // g
    nc, l = s // SSD_CHUNK, SSD_CHUNK
    x_dt = (xh.astype(jnp.float32) * dt[..., None]).reshape(b, nc, l, g, e, p)
    a_dt = (dt * a).reshape(b, nc, l, g, e)
    bc = bm.astype(jnp.float32).reshape(b, nc, l, g, n)
    cc = cm.astype(jnp.float32).reshape(b, nc, l, g, n)
    a_cs = jnp.cumsum(a_dt, axis=2)
    diff = a_cs[:, :, :, None] - a_cs[:, :, None, :]
    mask = jnp.tril(jnp.ones((l, l), dtype=bool))[:, :, None, None]
    decay = jnp.exp(jnp.where(mask, diff, -jnp.inf))
    cb = jnp.einsum('bclgn,bcsgn->bclsg', cc, bc)
    y_diag = jnp.einsum('bclsge,bcsgep->bclgep', cb[..., None] * decay, x_dt)
    decay_to_end = jnp.exp(a_cs[:, :, -1:] - a_cs)
    chunk_states = jnp.einsum('bclgn,bclgep->bcgepn', bc, x_dt * decay_to_end[..., None])
    chunk_decay = jnp.exp(a_cs[:, :, -1])

    def step(state, inp):
        st, dec = inp
        return dec[..., None, None] * state + st, state

    init = jnp.zeros((b, g, e, p, n), jnp.float32)
    _, states_in = lax.scan(step, init, (jnp.moveaxis(chunk_states, 1, 0), jnp.moveaxis(chunk_decay, 1, 0)))
    states_in = jnp.moveaxis(states_in, 0, 1)
    y_off = jnp.einsum('bclgn,bcgepn->bclgep', cc, states_in) * jnp.exp(a_cs)[..., None]
    return (y_diag + y_off).reshape(b, s, h, p)


def mla_attention(h_qa, h_kva, q_norm_g, w_q_b, kv_norm_g, w_kv_b, cos, sin):
    b, s, _ = h_qa.shape
    q = (rmsnorm(h_qa, q_norm_g) @ w_q_b).reshape(b, s, MLA_HEADS, QK_NOPE + QK_ROPE)
    q_nope = q[..., :QK_NOPE]
    q_rope = rope(q[..., QK_NOPE:], cos[:, :, None, :], sin[:, :, None, :])
    c_kv = rmsnorm(h_kva[..., :KV_LORA], kv_norm_g)
    k_rope = rope(h_kva[..., KV_LORA:], cos, sin)
    kv = (c_kv @ w_kv_b).reshape(b, s, MLA_HEADS, QK_NOPE + V_DIM)
    k_nope, v = kv[..., :QK_NOPE], kv[..., QK_NOPE:]
    scale = (QK_NOPE + QK_ROPE) ** -0.5
    nb = s // Q_BLOCK

    def to_blocks(t):
        return jnp.moveaxis(t.reshape(b, nb, Q_BLOCK, *t.shape[2:]), 1, 0)

    def attend(qb):
        qn, qr = qb
        sc = jnp.einsum('bqhd,bkhd->bhqk', qn, k_nope) + jnp.einsum('bqhr,bkr->bhqk', qr, k_rope)
        pr = jax.nn.softmax(sc.astype(jnp.float32) * scale, axis=-1).astype(v.dtype)
        return jnp.einsum('bhqk,bkhd->bqhd', pr, v)

    o = lax.map(attend, (to_blocks(q_nope), to_blocks(q_rope)))
    return jnp.moveaxis(o, 0, 1).reshape(b, s, ATTN_WIDTH)


def hybrid_mixer(h, w_in, conv_w, conv_b, dt_bias, a_log, d_skip, ssd_norm_g,
                 q_norm_g, w_q_b, kv_norm_g, w_kv_b, attn_norm_g, w_out, cos, sin):
    b, s, _ = h.shape
    cuts = [int(i) for i in np.cumsum(IN_SPLITS)[:-1]]
    z, xbc, dt_raw, h_qa, h_kva = jnp.split(h @ w_in, cuts, axis=-1)
    xbc = jax.nn.silu(dwconv_centred(xbc, conv_w, conv_b))
    xs, bm, cm = jnp.split(xbc, [SSD_INNER, SSD_INNER + SSD_GROUPS * SSD_STATE], axis=-1)
    xh = xs.reshape(b, s, SSD_HEADS, SSD_HEADDIM)
    bm = bm.reshape(b, s, SSD_GROUPS, SSD_STATE)
    cm = cm.reshape(b, s, SSD_GROUPS, SSD_STATE)
    dt = jax.nn.softplus(dt_raw.astype(jnp.float32).reshape(b, s, 2, SSD_HEADS) + dt_bias.astype(jnp.float32))
    a = -jnp.exp(a_log.astype(jnp.float32))
    y_fwd = ssd_chunked(xh, dt[:, :, 0], a[0], bm, cm)
    flip = lambda t: jnp.flip(t, axis=1)
    y_bwd = flip(ssd_chunked(flip(xh), flip(dt[:, :, 1]), a[1], flip(bm), flip(cm)))
    y = y_fwd + y_bwd + d_skip.astype(jnp.float32)[:, None] * xh.astype(jnp.float32)
    y = y.reshape(b, s, SSD_INNER) * jax.nn.silu(z.astype(jnp.float32))
    y = rmsnorm(y.reshape(b, s, SSD_GROUPS, SSD_INNER // SSD_GROUPS),
                ssd_norm_g.reshape(SSD_GROUPS, SSD_INNER // SSD_GROUPS))
    ssd_out = y.reshape(b, s, SSD_INNER).astype(h.dtype)
    attn_out = rmsnorm(mla_attention(h_qa, h_kva, q_norm_g, w_q_b, kv_norm_g, w_kv_b, cos, sin), attn_norm_g)
    return jnp.concatenate([ssd_out, attn_out.astype(h.dtype)], axis=-1) @ w_out


def setup_inputs(seed: int = 0) -> dict:
    key = jax.random.key(seed)
    ks = jax.random.split(key, 24)
    nrm = lambda k, shape, sc: jax.random.normal(k, shape, jnp.float32) * sc
    gain = lambda k, shape: 1.0 + 0.02 * jax.random.normal(k, shape, jnp.float32)
    x = nrm(ks[0], (BATCH, SEQ, D_MODEL), 1.0)
    c = nrm(ks[1], (BATCH, D_MODEL), 1.0)
    offset = jax.random.randint(ks[2], (BATCH, 1), 0, 1024, dtype=jnp.int32)
    positions = (offset + jnp.arange(SEQ, dtype=jnp.int32)[None, :]).astype(jnp.int32)
    ada_w = nrm(ks[3], (DEPTH, D_MODEL, N_MOD * D_MODEL), 0.5 * D_MODEL ** -0.5)
    ada_b = nrm(ks[4], (DEPTH, N_MOD * D_MODEL), 0.01)
    norm_g = gain(ks[5], (DEPTH, 3, D_MODEL))
    ffn_w_gate = nrm(ks[6], (DEPTH, 2, D_MODEL, D_FF), D_MODEL ** -0.5)
    ffn_w_up = nrm(ks[7], (DEPTH, 2, D_MODEL, D_FF), D_MODEL ** -0.5)
    ffn_w_down = nrm(ks[8], (DEPTH, 2, D_FF, D_MODEL), D_FF ** -0.5)
    w_in = nrm(ks[9], (DEPTH, D_MODEL, IN_COLS), D_MODEL ** -0.5)
    conv_w = nrm(ks[10], (DEPTH, SSD_CONV, CONV_CH), SSD_CONV ** -0.5)
    conv_b = nrm(ks[11], (DEPTH, CONV_CH), 0.01)
    dt0 = jnp.exp(jax.random.uniform(ks[12], (DEPTH, 2, SSD_HEADS), jnp.float32, np.log(1e-3), np.log(1e-1)))
    dt_bias = dt0 + jnp.log(-jnp.expm1(-dt0))
    a_log = jnp.log(jax.random.uniform(ks[13], (DEPTH, 2, SSD_HEADS), jnp.float32, 1.0, 16.0))
    d_skip = gain(ks[14], (DEPTH, SSD_HEADS))
    ssd_norm_g = gain(ks[15], (DEPTH, SSD_INNER))
    q_norm_g = gain(ks[16], (DEPTH, Q_LORA))
    w_q_b = nrm(ks[17], (DEPTH, Q_LORA, MLA_HEADS * (QK_NOPE + QK_ROPE)), Q_LORA ** -0.5)
    kv_norm_g = gain(ks[18], (DEPTH, KV_LORA))
    w_kv_b = nrm(ks[19], (DEPTH, KV_LORA, MLA_HEADS * (QK_NOPE + V_DIM)), KV_LORA ** -0.5)
    attn_norm_g = gain(ks[20], (DEPTH, ATTN_WIDTH))
    w_out = nrm(ks[21], (DEPTH, MIX_WIDTH, D_MODEL), MIX_WIDTH ** -0.5)
    final_norm_g = gain(ks[22], (D_MODEL,))
    return {'x': x, 'c': c, 'positions': positions, 'ada_w': ada_w, 'ada_b': ada_b,
            'norm_g': norm_g, 'ffn_w_gate': ffn_w_gate, 'ffn_w_up': ffn_w_up,
            'ffn_w_down': ffn_w_down, 'w_in': w_in, 'conv_w': conv_w, 'conv_b': conv_b,
            'dt_bias': dt_bias, 'a_log': a_log, 'd_skip': d_skip, 'ssd_norm_g': ssd_norm_g,
            'q_norm_g': q_norm_g, 'w_q_b': w_q_b, 'kv_norm_g': kv_norm_g, 'w_kv_b': w_kv_b,
            'attn_norm_g': attn_norm_g, 'w_out': w_out, 'final_norm_g': final_norm_g}


def reference(x, c, positions, ada_w, ada_b, norm_g, ffn_w_gate, ffn_w_up, ffn_w_down,
              w_in, conv_w, conv_b, dt_bias, a_log, d_skip, ssd_norm_g, q_norm_g, w_q_b,
              kv_norm_g, w_kv_b, attn_norm_g, w_out, final_norm_g):
    b, s, d = x.shape
    inv_freq = 1.0 / (ROPE_THETA ** (jnp.arange(0, QK_ROPE, 2, dtype=jnp.float32) / QK_ROPE))
    ang = positions.astype(jnp.float32)[..., None] * inv_freq
    cos, sin = jnp.cos(ang), jnp.sin(ang)
    c_act = jax.nn.silu(c)
    for l in range(DEPTH):
        mod = (c_act @ ada_w[l] + ada_b[l]).reshape(b, N_MOD, d)
        h = modulate(x, norm_g[l, 0], mod[:, 0], mod[:, 1])
        x = x + 0.5 * mod[:, 2][:, None, :] * swiglu(h, ffn_w_gate[l, 0], ffn_w_up[l, 0], ffn_w_down[l, 0])
        h = modulate(x, norm_g[l, 1], mod[:, 3], mod[:, 4])
        x = x + mod[:, 5][:, None, :] * hybrid_mixer(
            h, w_in[l], conv_w[l], conv_b[l], dt_bias[l], a_log[l], d_skip[l], ssd_norm_g[l],
            q_norm_g[l], w_q_b[l], kv_norm_g[l], w_kv_b[l], attn_norm_g[l], w_out[l], cos, sin)
        h = modulate(x, norm_g[l, 2], mod[:, 6], mod[:, 7])
        x = x + 0.5 * mod[:, 8][:, None, :] * swiglu(h, ffn_w_gate[l, 1], ffn_w_up[l, 1], ffn_w_down[l, 1])
    return rmsnorm(x, final_norm_g)
```

```python
import functools
import math

import jax
import jax.numpy as jnp
import numpy as np
from jax import lax
from jax.experimental import pallas as pl
from jax.experimental.pallas import tpu as pltpu

F32 = jnp.float32
BF16 = jnp.bfloat16

D_MODEL = 1024
D_FF = 2816
N_MOD = 9
SSD_INNER = 1024
SSD_HEADDIM = 64
SSD_HEADS = 16
SSD_GROUPS = 2
SSD_STATE = 128
SSD_CONV = 5
SSD_CHUNK = 128
CONV_CH = SSD_INNER + 2 * SSD_GROUPS * SSD_STATE
MLA_HEADS = 8
QK_NOPE = 128
QK_ROPE = 64
V_DIM = 128
Q_LORA = 512
KV_LORA = 256
ROPE_THETA = 10000.0
EPS = 1e-6

LANES = 128
SUBLANES = 8
HEAD_PAD = 256
VMEM_LIMIT = 56 * 1024 * 1024

C_Z = 0
C_XBC = C_Z + SSD_INNER
C_DT = C_XBC + CONV_CH
C_Q = C_DT + LANES
C_KV = C_Q + Q_LORA
C_KRA = C_KV + KV_LORA
C_KRB = C_KRA + LANES
IN_COLS_EXT = C_KRB + LANES

TM_FFN = 512
TM_MIX = 256
TM_OUT = 512
TQ = 512
TK = 512
ROPE_TS = 1024
CONV_HALO = SUBLANES
Q_SCALE = (QK_NOPE + QK_ROPE) ** -0.5 * math.log2(math.e)


def _dot(a, b):
    return jnp.dot(a, b, preferred_element_type=F32)


def _dot_nt(a, b):
    return lax.dot_general(a, b, (((1,), (1,)), ((), ())), preferred_element_type=F32)


def _rms(x, g):
    return x * lax.rsqrt(jnp.mean(x * x, axis=-1, keepdims=True) + EPS) * g


def _modulate(x, g, shift, scale):
    return _rms(x, g) * (1.0 + scale) + shift


def _sigmoid(x):
    return 1.0 / (1.0 + jnp.exp(-x))


def _const_spec(shape):
    nd = len(shape)
    return pl.BlockSpec(shape, lambda *_: (0,) * nd, pipeline_mode=pl.Buffered(1))


def _ada_kernel(c_ref, w_ref, b_ref, o_ref):
    c = c_ref[...]
    c_act = (c * _sigmoid(c)).astype(BF16)
    o_ref[0] = _dot(c_act, w_ref[0].astype(BF16)) + b_ref[0]


def _ada(c_pad, ada_w, ada_b):
    depth, d, n = ada_w.shape
    tn = 1536
    return pl.pallas_call(
        _ada_kernel,
        out_shape=jax.ShapeDtypeStruct((depth, SUBLANES, n), F32),
        grid=(depth, n // tn),
        in_specs=[
            pl.BlockSpec((SUBLANES, d), lambda l, j: (0, 0)),
            pl.BlockSpec((1, d, tn), lambda l, j: (l, 0, j)),
            pl.BlockSpec((1, 1, tn), lambda l, j: (l, 0, j)),
        ],
        out_specs=pl.BlockSpec((1, SUBLANES, tn), lambda l, j: (l, 0, j)),
        compiler_params=pltpu.CompilerParams(
            dimension_semantics=("arbitrary", "arbitrary"), vmem_limit_bytes=VMEM_LIMIT),
        name="ada",
    )(c_pad, ada_w, ada_b.reshape(depth, 1, n))


def _rope_kernel(pos_ref, inv_ref, cos_t_ref, sin_t_ref, cos_n_ref, sin_n_ref):
    ang = inv_ref[...] * pos_ref[0].astype(F32)
    c, s = jnp.cos(ang), jnp.sin(ang)
    zero = jnp.zeros((LANES - QK_ROPE, ang.shape[1]), F32)
    c_t = jnp.concatenate([c, c, zero], axis=0)
    s_t = jnp.concatenate([s, s, zero], axis=0)
    cos_t_ref[0] = c_t
    sin_t_ref[0] = s_t
    cos_n_ref[0] = c_t.T
    sin_n_ref[0] = s_t.T


def _rope_tables(positions):
    b, s = positions.shape
    inv_freq = 1.0 / (ROPE_THETA ** (jnp.arange(0, QK_ROPE, 2, dtype=F32) / QK_ROPE))
    t_shape = jax.ShapeDtypeStruct((b, LANES, s), F32)
    n_shape = jax.ShapeDtypeStruct((b, s, LANES), F32)
    t_spec = pl.BlockSpec((1, LANES, ROPE_TS), lambda i, j: (i, 0, j))
    n_spec = pl.BlockSpec((1, ROPE_TS, LANES), lambda i, j: (i, j, 0))
    return pl.pallas_call(
        _rope_kernel,
        out_shape=(t_shape, t_shape, n_shape, n_shape),
        grid=(b, s // ROPE_TS),
        in_specs=[
            pl.BlockSpec((1, 1, ROPE_TS), lambda i, j: (i, 0, j)),
            pl.BlockSpec((QK_ROPE // 2, 1), lambda i, j: (0, 0)),
        ],
        out_specs=(t_spec, t_spec, n_spec, n_spec),
        compiler_params=pltpu.CompilerParams(dimension_semantics=("arbitrary", "arbitrary")),
        name="rope_tables",
    )(positions.reshape(b, 1, s), inv_freq.reshape(QK_ROPE // 2, 1))


def _ffn_kernel(x_ref, mod_ref, g_ref, wg_ref, wu_ref, wd_ref, fg_ref, o_ref, *, row0, final):
    x = x_ref[0]
    shift = mod_ref[0, row0:row0 + 1, :]
    scale = mod_ref[0, row0 + 1:row0 + 2, :]
    gate = mod_ref[0, row0 + 2:row0 + 3, :]
    h = _modulate(x, g_ref[...], shift, scale).astype(BF16)
    a = _dot(h, wg_ref[...])
    u = _dot(h, wu_ref[...])
    act = (a * _sigmoid(a) * u).astype(BF16)
    out = x + 0.5 * gate * _dot(act, wd_ref[...])
    if final:
        out = _rms(out, fg_ref[...])
    o_ref[0] = out


def _ffn(x, mod, g, wg, wu, wd, fg, *, row0, final):
    b, s, d = x.shape
    tile = pl.BlockSpec((1, TM_FFN, d), lambda i, j: (i, j, 0))
    return pl.pallas_call(
        functools.partial(_ffn_kernel, row0=row0, final=final),
        out_shape=jax.ShapeDtypeStruct(x.shape, F32),
        grid=(b, s // TM_FFN),
        in_specs=[
            tile,
            pl.BlockSpec((1, N_MOD, d), lambda i, j: (i, 0, 0)),
            _const_spec((1, d)),
            _const_spec(wg.shape),
            _const_spec(wu.shape),
            _const_spec(wd.shape),
            _const_spec((1, d)),
        ],
        out_specs=tile,
        compiler_params=pltpu.CompilerParams(
            dimension_semantics=("arbitrary", "arbitrary"), vmem_limit_bytes=VMEM_LIMIT),
        name="ffn",
    )(x, mod, g, wg, wu, wd, fg)


def _mix_in_kernel(x_ref, xp_ref, xn_ref, mod_ref, g_ref, win_ref, cw_ref, cb_ref, dtb_ref,
                   qg_ref, wq_ref, kvg_ref, wkn_ref, wv_ref,
                   cos_t_ref, sin_t_ref, cos_n_ref, sin_n_ref,
                   z_ref, xbc_ref, dt_ref, q_ref, k_ref, v_ref, u_scr):
    j = pl.program_id(1)
    nj = pl.num_programs(1)
    tm = x_ref.shape[1]
    g = g_ref[...]
    shift = mod_ref[0, 3:4, :]
    scale = mod_ref[0, 4:5, :]

    h = _modulate(x_ref[0], g, shift, scale).astype(BF16)
    proj = _dot(h, win_ref[...])

    z_ref[0] = proj[:, C_Z:C_XBC]
    dt_ref[0] = jax.nn.softplus(proj[:, C_DT:C_Q] + dtb_ref[...])

    x_halo = jnp.concatenate([xp_ref[0], xn_ref[0]], axis=0)
    h_halo = _modulate(x_halo, g, shift, scale).astype(BF16)
    u_halo = _dot(h_halo, win_ref[:, C_XBC:C_DT])
    u_scr[0:CONV_HALO, :] = jnp.where(j > 0, u_halo[:CONV_HALO], 0.0)
    u_scr[CONV_HALO:CONV_HALO + tm, :] = proj[:, C_XBC:C_DT]
    u_scr[CONV_HALO + tm:, :] = jnp.where(j < nj - 1, u_halo[CONV_HALO:], 0.0)
    acc = jnp.broadcast_to(cb_ref[...], (tm, CONV_CH))
    for k in range(SSD_CONV):
        off = CONV_HALO - SSD_CONV // 2 + k
        acc = acc + cw_ref[k:k + 1, :] * u_scr[off:off + tm, :]
    xbc_ref[0] = acc * _sigmoid(acc)

    hq = _rms(proj[:, C_Q:C_KV], qg_ref[...]).astype(BF16)
    q_all = _dot_nt(wq_ref[...], hq)
    ckv = _rms(proj[:, C_KV:C_KRA], kvg_ref[...]).astype(BF16)
    k_nope = _dot(ckv, wkn_ref[...])
    v_all = _dot_nt(wv_ref[...], ckv)
    k_rope = (proj[:, C_KRA:C_KRB] * cos_n_ref[0] + proj[:, C_KRB:] * sin_n_ref[0]).astype(BF16)
    cos_t = cos_t_ref[0]
    sin_t = sin_t_ref[0]
    n_rows = MLA_HEADS * LANES
    for hd in range(MLA_HEADS):
        r = slice(hd * LANES, (hd + 1) * LANES)
        q_ref[0, hd, 0:LANES, :] = (q_all[r] * Q_SCALE).astype(BF16)
        q_rot = q_all[n_rows + hd * LANES:n_rows + (hd + 1) * LANES] * cos_t \
            + q_all[2 * n_rows + hd * LANES:2 * n_rows + (hd + 1) * LANES] * sin_t
        q_ref[0, hd, LANES:, :] = (q_rot * Q_SCALE).astype(BF16)
        k_ref[0, hd, :, 0:LANES] = k_nope[:, r].astype(BF16)
        k_ref[0, hd, :, LANES:] = k_rope
        v_ref[0, hd] = v_all[r].astype(BF16)


def _mix_in(x, mod, g, win, cw, cb, dtb, qg, wq, kvg, wkn, wv, tabs):
    b, s, d = x.shape
    tm = TM_MIX
    cos_t, sin_t, cos_n, sin_n = tabs
    nblk = s // CONV_HALO
    per = tm // CONV_HALO
    out_shape = (
        jax.ShapeDtypeStruct((b, s, SSD_INNER), F32),
        jax.ShapeDtypeStruct((b, s, CONV_CH), F32),
        jax.ShapeDtypeStruct((b, s, LANES), F32),
        jax.ShapeDtypeStruct((b, MLA_HEADS, HEAD_PAD, s), BF16),
        jax.ShapeDtypeStruct((b, MLA_HEADS, s, HEAD_PAD), BF16),
        jax.ShapeDtypeStruct((b, MLA_HEADS, V_DIM, s), BF16),
    )
    tok = lambda w: pl.BlockSpec((1, tm, w), lambda i, j: (i, j, 0))
    in_specs = [
        tok(d),
        pl.BlockSpec((1, CONV_HALO, d), lambda i, j: (i, jnp.maximum(j * per - 1, 0), 0)),
        pl.BlockSpec((1, CONV_HALO, d), lambda i, j: (i, jnp.minimum((j + 1) * per, nblk - 1), 0)),
        pl.BlockSpec((1, N_MOD, d), lambda i, j: (i, 0, 0)),
        _const_spec((1, d)),
        _const_spec(win.shape),
        _const_spec(cw.shape),
        _const_spec(cb.shape),
        _const_spec(dtb.shape),
        _const_spec(qg.shape),
        _const_spec(wq.shape),
        _const_spec(kvg.shape),
        _const_spec(wkn.shape),
        _const_spec(wv.shape),
        pl.BlockSpec((1, LANES, tm), lambda i, j: (i, 0, j)),
        pl.BlockSpec((1, LANES, tm), lambda i, j: (i, 0, j)),
        tok(LANES),
        tok(LANES),
    ]
    out_specs = (
        tok(SSD_INNER),
        tok(CONV_CH),
        tok(LANES),
        pl.BlockSpec((1, MLA_HEADS, HEAD_PAD, tm), lambda i, j: (i, 0, 0, j)),
        pl.BlockSpec((1, MLA_HEADS, tm, HEAD_PAD), lambda i, j: (i, 0, j, 0)),
        pl.BlockSpec((1, MLA_HEADS, V_DIM, tm), lambda i, j: (i, 0, 0, j)),
    )
    return pl.pallas_call(
        _mix_in_kernel,
        out_shape=out_shape,
        grid=(b, s // tm),
        in_specs=in_specs,
        out_specs=out_specs,
        scratch_shapes=[pltpu.VMEM((tm + 2 * CONV_HALO, CONV_CH), F32)],
        compiler_params=pltpu.CompilerParams(
            dimension_semantics=("arbitrary", "arbitrary"), vmem_limit_bytes=VMEM_LIMIT),
        name="mix_in",
    )(x, x, x, mod, g, win, cw, cb, dtb, qg, wq, kvg, wkn, wv, cos_t, sin_t, cos_n, sin_n)


def _split3(x):
    hi = x.astype(BF16)
    r1 = x - hi.astype(F32)
    mid = r1.astype(BF16)
    lo = (r1 - mid.astype(F32)).astype(BF16)
    return hi, mid, lo


def _ssd_direction(xbc, dt, a_row, state_ref, y_ref, d_row, *, backward):
    L = SSD_CHUNK
    off = SSD_HEADS if backward else 0
    xs = xbc[:, :SSD_INNER]
    row_i = lax.broadcasted_iota(jnp.int32, (L, L), 0)
    col_i = lax.broadcasted_iota(jnp.int32, (L, L), 1)
    keep = (row_i <= col_i) if backward else (row_i >= col_i)
    tri = jnp.where(keep, 1.0, 0.0).astype(BF16)
    a_dt = dt * a_row
    hi, mid, lo = _split3(a_dt)
    a_cs = _dot(tri, hi) + _dot(tri, mid) + _dot(tri, lo)
    a_cs_t = a_cs.T
    dt_t = dt.T
    edge = 0 if backward else L - 1
    for grp in range(SSD_GROUPS):
        b_g = xbc[:, SSD_INNER + grp * SSD_STATE:SSD_INNER + (grp + 1) * SSD_STATE]
        c_off = SSD_INNER + SSD_GROUPS * SSD_STATE
        c_g = xbc[:, c_off + grp * SSD_STATE:c_off + (grp + 1) * SSD_STATE]
        cb = _dot_nt(c_g.astype(BF16), b_g.astype(BF16))
        b_t = b_g.T
        for e in range(SSD_HEADS // SSD_GROUPS):
            hd = grp * (SSD_HEADS // SSD_GROUPS) + e
            col = a_cs[:, off + hd:off + hd + 1]
            row = a_cs_t[off + hd:off + hd + 1, :]
            dt_row = dt_t[off + hd:off + hd + 1, :]
            decay = jnp.exp(jnp.where(keep, col - row, -jnp.inf))
            m_h = (cb * decay * dt_row).astype(BF16)
            c_e = (c_g * jnp.exp(col)).astype(BF16)
            state = state_ref[hd]
            xs_f = xs[:, hd * SSD_HEADDIM:(hd + 1) * SSD_HEADDIM]
            xs_h = xs_f.astype(BF16)
            lhs = jnp.concatenate([m_h, c_e], axis=1)
            rhs = jnp.concatenate([xs_h, state.astype(BF16)], axis=0)
            y_h = _dot(lhs, rhs)
            if d_row is not None:
                y_h = y_h + d_row[:, hd * SSD_HEADDIM:(hd + 1) * SSD_HEADDIM] * xs_f
            y_ref[0, :, hd * SSD_HEADDIM:(hd + 1) * SSD_HEADDIM] = y_h
            a_tot = row[:, edge:edge + 1]
            w_row = dt_row * jnp.exp(a_tot - row)
            upd = _dot((b_t * w_row).astype(BF16), xs_h)
            state_ref[hd] = jnp.exp(a_tot) * state + upd


def _ssd_kernel(xf_ref, xb_ref, dtf_ref, dtb_ref, alog_ref, d_ref, yf_ref, yb_ref, sf_ref, sb_ref):
    @pl.when(pl.program_id(1) == 0)
    def _():
        sf_ref[...] = jnp.zeros_like(sf_ref)
        sb_ref[...] = jnp.zeros_like(sb_ref)

    a_row = -jnp.exp(alog_ref[...])
    _ssd_direction(xf_ref[0], dtf_ref[0], a_row, sf_ref, yf_ref, d_ref[...], backward=False)
    _ssd_direction(xb_ref[0], dtb_ref[0], a_row, sb_ref, yb_ref, None, backward=True)


def _ssd(xbc, dt, alog_pad, d_exp):
    b, s, _ = xbc.shape
    nc = s // SSD_CHUNK
    fwd = lambda w: pl.BlockSpec((1, SSD_CHUNK, w), lambda i, c: (i, c, 0))
    bwd = lambda w: pl.BlockSpec((1, SSD_CHUNK, w), lambda i, c: (i, nc - 1 - c, 0))
    y_shape = jax.ShapeDtypeStruct((b, s, SSD_INNER), F32)
    state = pltpu.VMEM((SSD_HEADS, SSD_STATE, SSD_HEADDIM), F32)
    return pl.pallas_call(
        _ssd_kernel,
        out_shape=(y_shape, y_shape),
        grid=(b, nc),
        in_specs=[fwd(CONV_CH), bwd(CONV_CH), fwd(LANES), bwd(LANES),
                  _const_spec((1, LANES)), _const_spec((1, SSD_INNER))],
        out_specs=(fwd(SSD_INNER), bwd(SSD_INNER)),
        scratch_shapes=[state, state],
        compiler_params=pltpu.CompilerParams(
            dimension_semantics=("arbitrary", "arbitrary"), vmem_limit_bytes=VMEM_LIMIT),
        name="ssd",
    )(xbc, xbc, dt, dt, alog_pad, d_exp)


def _attn_kernel(q_ref, k_ref, v_ref, o_ref):
    q_t = q_ref[0, 0]
    tq = q_t.shape[1]
    n_kv = k_ref.shape[2] // TK

    def step(i, carry):
        m, l, acc = carry
        start = pl.multiple_of(i * TK, TK)
        k_blk = k_ref[0, 0, pl.ds(start, TK), :]
        s_t = _dot(k_blk, q_t)
        m_new = jnp.maximum(m, jnp.max(s_t, axis=0, keepdims=True))
        alpha = jnp.exp2(m - m_new)
        p_t = jnp.exp2(s_t - m_new)
        l_new = alpha * l + jnp.sum(p_t, axis=0, keepdims=True)
        v_blk = v_ref[0, 0, :, pl.ds(start, TK)]
        acc_new = alpha * acc + _dot(v_blk, p_t.astype(BF16))
        return m_new, l_new, acc_new

    m0 = jnp.full((1, tq), -jnp.inf, F32)
    l0 = jnp.zeros((1, tq), F32)
    acc0 = jnp.zeros((V_DIM, tq), F32)
    _, l, acc = lax.fori_loop(0, n_kv, step, (m0, l0, acc0))
    o_ref[0] = (acc / l).T


def _attention(q_t, k, v_t):
    b, nh, _, s = q_t.shape
    return pl.pallas_call(
        _attn_kernel,
        out_shape=jax.ShapeDtypeStruct((b, s, nh * V_DIM), F32),
        grid=(b, nh, s // TQ),
        in_specs=[
            pl.BlockSpec((1, 1, HEAD_PAD, TQ), lambda i, h, j: (i, h, 0, j)),
            pl.BlockSpec((1, 1, s, HEAD_PAD), lambda i, h, j: (i, h, 0, 0)),
            pl.BlockSpec((1, 1, V_DIM, s), lambda i, h, j: (i, h, 0, 0)),
        ],
        out_specs=pl.BlockSpec((1, TQ, V_DIM), lambda i, h, j: (i, j, h)),
        compiler_params=pltpu.CompilerParams(
            dimension_semantics=("arbitrary", "arbitrary", "arbitrary"),
            vmem_limit_bytes=VMEM_LIMIT),
        name="attention",
    )(q_t, k, v_t)


def _mix_out_kernel(x_ref, mod_ref, yf_ref, yb_ref, z_ref, at_ref, sg_ref, ag_ref,
                    wo_s_ref, wo_a_ref, o_ref):
    gate = mod_ref[0, 5:6, :]
    z = z_ref[0]
    y = (yf_ref[0] + yb_ref[0]) * (z * _sigmoid(z))
    gw = SSD_INNER // SSD_GROUPS
    parts = [_rms(y[:, i * gw:(i + 1) * gw], sg_ref[:, i * gw:(i + 1) * gw]).astype(BF16)
             for i in range(SSD_GROUPS)]
    ssd_out = jnp.concatenate(parts, axis=1)
    attn = _rms(at_ref[0], ag_ref[...]).astype(BF16)
    mixed = _dot(ssd_out, wo_s_ref[...]) + _dot(attn, wo_a_ref[...])
    o_ref[0] = x_ref[0] + gate * mixed


def _mix_out(x, mod, y_f, y_b, z, attn, sg, ag, wo_s, wo_a):
    b, s, d = x.shape
    tile = pl.BlockSpec((1, TM_OUT, d), lambda i, j: (i, j, 0))
    return pl.pallas_call(
        _mix_out_kernel,
        out_shape=jax.ShapeDtypeStruct(x.shape, F32),
        grid=(b, s // TM_OUT),
        in_specs=[tile, pl.BlockSpec((1, N_MOD, d), lambda i, j: (i, 0, 0)),
                  tile, tile, tile, tile,
                  _const_spec((1, d)), _const_spec((1, d)),
                  _const_spec(wo_s.shape), _const_spec(wo_a.shape)],
        out_specs=tile,
        compiler_params=pltpu.CompilerParams(
            dimension_semantics=("arbitrary", "arbitrary"), vmem_limit_bytes=VMEM_LIMIT),
        name="mix_out",
    )(x, mod, y_f, y_b, z, attn, sg, ag, wo_s, wo_a)


def _pad_cols(w, width):
    return jnp.pad(w, ((0, 0), (0, width - w.shape[1])))


def _prep_w_in(w_in):
    cuts = np.cumsum([SSD_INNER, CONV_CH, 2 * SSD_HEADS, Q_LORA, KV_LORA])
    w_z, w_xbc, w_dt, w_q, w_kv, w_kr = jnp.split(w_in, [int(c) for c in cuts], axis=1)
    t1, t2 = w_kr[:, :QK_ROPE // 2], w_kr[:, QK_ROPE // 2:]
    kra = _pad_cols(jnp.concatenate([t1, t2], axis=1), LANES)
    krb = _pad_cols(jnp.concatenate([-t2, t1], axis=1), LANES)
    return jnp.concatenate([w_z, w_xbc, _pad_cols(w_dt, LANES), w_q, w_kv, kra, krb],
                           axis=1).astype(BF16)


def _prep_w_q(w_q_b):
    w = w_q_b.reshape(Q_LORA, MLA_HEADS, QK_NOPE + QK_ROPE)
    nope = w[:, :, :QK_NOPE]
    t1 = w[:, :, QK_NOPE:QK_NOPE + QK_ROPE // 2]
    t2 = w[:, :, QK_NOPE + QK_ROPE // 2:]
    zero = jnp.zeros((Q_LORA, MLA_HEADS, LANES - QK_ROPE), w.dtype)
    ra = jnp.concatenate([t1, t2, zero], axis=2)
    rb = jnp.concatenate([-t2, t1, zero], axis=2)
    flat = lambda t: t.reshape(Q_LORA, MLA_HEADS * LANES)
    return jnp.concatenate([flat(nope), flat(ra), flat(rb)], axis=1).T.astype(BF16)


def _prep_w_kv(w_kv_b):
    w = w_kv_b.reshape(KV_LORA, MLA_HEADS, QK_NOPE + V_DIM)
    w_kn = w[:, :, :QK_NOPE].reshape(KV_LORA, MLA_HEADS * QK_NOPE).astype(BF16)
    w_v_t = w[:, :, QK_NOPE:].reshape(KV_LORA, MLA_HEADS * V_DIM).T.astype(BF16)
    return w_kn, w_v_t


def kernel(x, c, positions, ada_w, ada_b, norm_g, ffn_w_gate, ffn_w_up, ffn_w_down, w_in, conv_w,
           conv_b, dt_bias, a_log, d_skip, ssd_norm_g, q_norm_g, w_q_b, kv_norm_g, w_kv_b,
           attn_norm_g, w_out, final_norm_g):
    b, s, d = x.shape
    depth = ada_w.shape[0]
    assert (b * s) % TM_FFN == 0 and s % TQ == 0 and s % TK == 0 and s % SSD_CHUNK == 0

    c_pad = jnp.pad(c, ((0, SUBLANES - b), (0, 0)))
    mod_all = _ada(c_pad, ada_w, ada_b)
    tabs = _rope_tables(positions)
    fg = final_norm_g.reshape(1, d)

    for l in range(depth):
        mod = mod_all[l, :b].reshape(b, N_MOD, d)
        bf = lambda w: w.astype(BF16)
        x = _ffn(x, mod, norm_g[l, 0].reshape(1, d), bf(ffn_w_gate[l, 0]), bf(ffn_w_up[l, 0]),
                 bf(ffn_w_down[l, 0]), fg, row0=0, final=False)

        w_kn, w_v_t = _prep_w_kv(w_kv_b[l])
        dtb = _pad_cols(dt_bias[l].reshape(1, 2 * SSD_HEADS), LANES)
        z, xbc, dt, q_t, k, v_t = _mix_in(
            x, mod, norm_g[l, 1].reshape(1, d), _prep_w_in(w_in[l]), conv_w[l],
            conv_b[l].reshape(1, CONV_CH), dtb, q_norm_g[l].reshape(1, Q_LORA), _prep_w_q(w_q_b[l]),
            kv_norm_g[l].reshape(1, KV_LORA), w_kn, w_v_t, tabs)

        alog_pad = _pad_cols(a_log[l].reshape(1, 2 * SSD_HEADS), LANES)
        d_exp = jnp.repeat(d_skip[l], SSD_HEADDIM).reshape(1, SSD_INNER)
        y_f, y_b = _ssd(xbc, dt, alog_pad, d_exp)
        attn = _attention(q_t, k, v_t)

        x = _mix_out(x, mod, y_f, y_b, z, attn, ssd_norm_g[l].reshape(1, SSD_INNER),
                     attn_norm_g[l].reshape(1, MLA_HEADS * V_DIM),
                     bf(w_out[l, :SSD_INNER]), bf(w_out[l, SSD_INNER:]))

        x = _ffn(x, mod, norm_g[l, 2].reshape(1, d), bf(ffn_w_gate[l, 1]), bf(ffn_w_up[l, 1]),
                 bf(ffn_w_down[l, 1]), fg, row0=6, final=(l == depth - 1))
    return x
```

```python
import functools
import math

import jax
import jax.numpy as jnp
import numpy as np
from jax import lax
from jax.experimental import pallas as pl
from jax.experimental.pallas import tpu as pltpu

F32 = jnp.float32
BF16 = jnp.bfloat16

D_MODEL = 1024
D_FF = 2816
N_MOD = 9
SSD_INNER = 1024
SSD_HEADDIM = 64
SSD_HEADS = 16
SSD_GROUPS = 2
SSD_STATE = 128
SSD_CONV = 5
SSD_CHUNK = 128
CONV_CH = SSD_INNER + 2 * SSD_GROUPS * SSD_STATE
MLA_HEADS = 8
QK_NOPE = 128
QK_ROPE = 64
V_DIM = 128
Q_LORA = 512
KV_LORA = 256
ROPE_THETA = 10000.0
EPS = 1e-6

LANES = 128
SUBLANES = 8
HEAD_PAD = 256
VMEM_LIMIT = 56 * 1024 * 1024

C_Z = 0
C_XBC = C_Z + SSD_INNER
C_DT = C_XBC + CONV_CH
C_Q = C_DT + LANES
C_KV = C_Q + Q_LORA
C_KRA = C_KV + KV_LORA
C_KRB = C_KRA + LANES
IN_COLS_EXT = C_KRB + LANES

TM_FFN = 512
TM_MIX = 256
TM_OUT = 512
TQ = 512
TK_QK = 2048
TK_PV = 2048
ROPE_TS = 1024
CONV_HALO = SUBLANES
Q_SCALE = (QK_NOPE + QK_ROPE) ** -0.5 * math.log2(math.e)


def _dot(a, b):
    return jnp.dot(a, b, preferred_element_type=F32)


def _dot_nt(a, b):
    return lax.dot_general(a, b, (((1,), (1,)), ((), ())), preferred_element_type=F32)


def _rms(x, g):
    return x * lax.rsqrt(jnp.mean(x * x, axis=-1, keepdims=True) + EPS) * g


def _modulate(x, g, shift, scale):
    return _rms(x, g) * (1.0 + scale) + shift


def _sigmoid(x):
    return 1.0 / (1.0 + jnp.exp(-x))


def _const_spec(shape):
    nd = len(shape)
    return pl.BlockSpec(shape, lambda *_: (0,) * nd, pipeline_mode=pl.Buffered(1))


def _ada_kernel(c_ref, w_ref, b_ref, o_ref):
    c = c_ref[...]
    c_act = (c * _sigmoid(c)).astype(BF16)
    o_ref[0] = _dot(c_act, w_ref[0].astype(BF16)) + b_ref[0]


def _ada(c_pad, ada_w, ada_b):
    depth, d, n = ada_w.shape
    tn = 1536
    return pl.pallas_call(
        _ada_kernel,
        out_shape=jax.ShapeDtypeStruct((depth, SUBLANES, n), F32),
        grid=(depth, n // tn),
        in_specs=[
            pl.BlockSpec((SUBLANES, d), lambda l, j: (0, 0)),
            pl.BlockSpec((1, d, tn), lambda l, j: (l, 0, j)),
            pl.BlockSpec((1, 1, tn), lambda l, j: (l, 0, j)),
        ],
        out_specs=pl.BlockSpec((1, SUBLANES, tn), lambda l, j: (l, 0, j)),
        compiler_params=pltpu.CompilerParams(
            dimension_semantics=("arbitrary", "arbitrary"), vmem_limit_bytes=VMEM_LIMIT),
        name="ada",
    )(c_pad, ada_w, ada_b.reshape(depth, 1, n))


def _rope_kernel(pos_ref, inv_ref, cos_t_ref, sin_t_ref, cos_n_ref, sin_n_ref):
    ang = inv_ref[...] * pos_ref[0].astype(F32)
    c, s = jnp.cos(ang), jnp.sin(ang)
    zero = jnp.zeros((LANES - QK_ROPE, ang.shape[1]), F32)
    c_t = jnp.concatenate([c, c, zero], axis=0)
    s_t = jnp.concatenate([s, s, zero], axis=0)
    cos_t_ref[0] = c_t
    sin_t_ref[0] = s_t
    cos_n_ref[0] = c_t.T
    sin_n_ref[0] = s_t.T


def _rope_tables(positions):
    b, s = positions.shape
    inv_freq = 1.0 / (ROPE_THETA ** (jnp.arange(0, QK_ROPE, 2, dtype=F32) / QK_ROPE))
    t_shape = jax.ShapeDtypeStruct((b, LANES, s), F32)
    n_shape = jax.ShapeDtypeStruct((b, s, LANES), F32)
    t_spec = pl.BlockSpec((1, LANES, ROPE_TS), lambda i, j: (i, 0, j))
    n_spec = pl.BlockSpec((1, ROPE_TS, LANES), lambda i, j: (i, j, 0))
    return pl.pallas_call(
        _rope_kernel,
        out_shape=(t_shape, t_shape, n_shape, n_shape),
        grid=(b, s // ROPE_TS),
        in_specs=[
            pl.BlockSpec((1, 1, ROPE_TS), lambda i, j: (i, 0, j)),
            pl.BlockSpec((QK_ROPE // 2, 1), lambda i, j: (0, 0)),
        ],
        out_specs=(t_spec, t_spec, n_spec, n_spec),
        compiler_params=pltpu.CompilerParams(dimension_semantics=("arbitrary", "arbitrary")),
        name="rope_tables",
    )(positions.reshape(b, 1, s), inv_freq.reshape(QK_ROPE // 2, 1))


def _ffn_kernel(x_ref, mod_ref, g_ref, wg_ref, wu_ref, wd_ref, fg_ref, o_ref, *, row0, final):
    x = x_ref[0]
    shift = mod_ref[0, row0:row0 + 1, :]
    scale = mod_ref[0, row0 + 1:row0 + 2, :]
    gate = mod_ref[0, row0 + 2:row0 + 3, :]
    h = _modulate(x, g_ref[...], shift, scale).astype(BF16)
    a = _dot(h, wg_ref[...])
    u = _dot(h, wu_ref[...])
    act = (a * _sigmoid(a) * u).astype(BF16)
    out = x + 0.5 * gate * _dot(act, wd_ref[...])
    if final:
        out = _rms(out, fg_ref[...])
    o_ref[0] = out


def _ffn(x, mod, g, wg, wu, wd, fg, *, row0, final):
    b, s, d = x.shape
    tile = pl.BlockSpec((1, TM_FFN, d), lambda i, j: (i, j, 0))
    return pl.pallas_call(
        functools.partial(_ffn_kernel, row0=row0, final=final),
        out_shape=jax.ShapeDtypeStruct(x.shape, F32),
        grid=(b, s // TM_FFN),
        in_specs=[
            tile,
            pl.BlockSpec((1, N_MOD, d), lambda i, j: (i, 0, 0)),
            _const_spec((1, d)),
            _const_spec(wg.shape),
            _const_spec(wu.shape),
            _const_spec(wd.shape),
            _const_spec((1, d)),
        ],
        out_specs=tile,
        compiler_params=pltpu.CompilerParams(
            dimension_semantics=("arbitrary", "arbitrary"), vmem_limit_bytes=VMEM_LIMIT),
        name="ffn",
    )(x, mod, g, wg, wu, wd, fg)


def _mix_in_kernel(x_ref, xp_ref, xn_ref, mod_ref, g_ref, win_ref, cw_ref, cb_ref, dtb_ref,
                   qg_ref, wq_ref, kvg_ref, wkn_ref, wv_ref,
                   cos_t_ref, sin_t_ref, cos_n_ref, sin_n_ref,
                   z_ref, xbc_ref, dt_ref, q_ref, k_ref, v_ref, u_scr):
    j = pl.program_id(1)
    nj = pl.num_programs(1)
    tm = x_ref.shape[1]
    g = g_ref[...]
    shift = mod_ref[0, 3:4, :]
    scale = mod_ref[0, 4:5, :]

    h = _modulate(x_ref[0], g, shift, scale).astype(BF16)
    proj = _dot(h, win_ref[...])

    z_ref[0] = proj[:, C_Z:C_XBC]
    dt_ref[0] = jax.nn.softplus(proj[:, C_DT:C_Q] + dtb_ref[...])

    x_halo = jnp.concatenate([xp_ref[0], xn_ref[0]], axis=0)
    h_halo = _modulate(x_halo, g, shift, scale).astype(BF16)
    u_halo = _dot(h_halo, win_ref[:, C_XBC:C_DT])
    u_scr[0:CONV_HALO, :] = jnp.where(j > 0, u_halo[:CONV_HALO], 0.0)
    u_scr[CONV_HALO:CONV_HALO + tm, :] = proj[:, C_XBC:C_DT]
    u_scr[CONV_HALO + tm:, :] = jnp.where(j < nj - 1, u_halo[CONV_HALO:], 0.0)
    acc = jnp.broadcast_to(cb_ref[...], (tm, CONV_CH))
    for k in range(SSD_CONV):
        off = CONV_HALO - SSD_CONV // 2 + k
        acc = acc + cw_ref[k:k + 1, :] * u_scr[off:off + tm, :]
    xbc_ref[0] = acc * _sigmoid(acc)

    hq = _rms(proj[:, C_Q:C_KV], qg_ref[...]).astype(BF16)
    q_all = _dot_nt(wq_ref[...], hq)
    ckv = _rms(proj[:, C_KV:C_KRA], kvg_ref[...]).astype(BF16)
    k_nope = _dot(ckv, wkn_ref[...])
    v_all = _dot_nt(wv_ref[...], ckv)
    k_rope = (proj[:, C_KRA:C_KRB] * cos_n_ref[0] + proj[:, C_KRB:] * sin_n_ref[0]).astype(BF16)
    cos_t = cos_t_ref[0]
    sin_t = sin_t_ref[0]
    n_rows = MLA_HEADS * LANES
    for hd in range(MLA_HEADS):
        r = slice(hd * LANES, (hd + 1) * LANES)
        q_ref[0, hd, 0:LANES, :] = (q_all[r] * Q_SCALE).astype(BF16)
        q_rot = q_all[n_rows + hd * LANES:n_rows + (hd + 1) * LANES] * cos_t \
            + q_all[2 * n_rows + hd * LANES:2 * n_rows + (hd + 1) * LANES] * sin_t
        q_ref[0, hd, LANES:, :] = (q_rot * Q_SCALE).astype(BF16)
        k_ref[0, hd, :, 0:LANES] = k_nope[:, r].astype(BF16)
        k_ref[0, hd, :, LANES:] = k_rope
        v_ref[0, hd] = v_all[r].astype(BF16)


def _mix_in(x, mod, g, win, cw, cb, dtb, qg, wq, kvg, wkn, wv, tabs):
    b, s, d = x.shape
    tm = TM_MIX
    cos_t, sin_t, cos_n, sin_n = tabs
    nblk = s // CONV_HALO
    per = tm // CONV_HALO
    out_shape = (
        jax.ShapeDtypeStruct((b, s, SSD_INNER), F32),
        jax.ShapeDtypeStruct((b, s, CONV_CH), F32),
        jax.ShapeDtypeStruct((b, s, LANES), F32),
        jax.ShapeDtypeStruct((b, MLA_HEADS, HEAD_PAD, s), BF16),
        jax.ShapeDtypeStruct((b, MLA_HEADS, s, HEAD_PAD), BF16),
        jax.ShapeDtypeStruct((b, MLA_HEADS, V_DIM, s), BF16),
    )
    tok = lambda w: pl.BlockSpec((1, tm, w), lambda i, j: (i, j, 0))
    in_specs = [
        tok(d),
        pl.BlockSpec((1, CONV_HALO, d), lambda i, j: (i, jnp.maximum(j * per - 1, 0), 0)),
        pl.BlockSpec((1, CONV_HALO, d), lambda i, j: (i, jnp.minimum((j + 1) * per, nblk - 1), 0)),
        pl.BlockSpec((1, N_MOD, d), lambda i, j: (i, 0, 0)),
        _const_spec((1, d)),
        _const_spec(win.shape),
        _const_spec(cw.shape),
        _const_spec(cb.shape),
        _const_spec(dtb.shape),
        _const_spec(qg.shape),
        _const_spec(wq.shape),
        _const_spec(kvg.shape),
        _const_spec(wkn.shape),
        _const_spec(wv.shape),
        pl.BlockSpec((1, LANES, tm), lambda i, j: (i, 0, j)),
        pl.BlockSpec((1, LANES, tm), lambda i, j: (i, 0, j)),
        tok(LANES),
        tok(LANES),
    ]
    out_specs = (
        tok(SSD_INNER),
        tok(CONV_CH),
        tok(LANES),
        pl.BlockSpec((1, MLA_HEADS, HEAD_PAD, tm), lambda i, j: (i, 0, 0, j)),
        pl.BlockSpec((1, MLA_HEADS, tm, HEAD_PAD), lambda i, j: (i, 0, j, 0)),
        pl.BlockSpec((1, MLA_HEADS, V_DIM, tm), lambda i, j: (i, 0, 0, j)),
    )
    return pl.pallas_call(
        _mix_in_kernel,
        out_shape=out_shape,
        grid=(b, s // tm),
        in_specs=in_specs,
        out_specs=out_specs,
        scratch_shapes=[pltpu.VMEM((tm + 2 * CONV_HALO, CONV_CH), F32)],
        compiler_params=pltpu.CompilerParams(
            dimension_semantics=("arbitrary", "arbitrary"), vmem_limit_bytes=VMEM_LIMIT),
        name="mix_in",
    )(x, x, x, mod, g, win, cw, cb, dtb, qg, wq, kvg, wkn, wv, cos_t, sin_t, cos_n, sin_n)


def _split3(x):
    hi = x.astype(BF16)
    r1 = x - hi.astype(F32)
    mid = r1.astype(BF16)
    lo = (r1 - mid.astype(F32)).astype(BF16)
    return hi, mid, lo


def _ssd_direction(xbc, dt, a_row, state_ref, y_ref, d_row, *, backward):
    L = SSD_CHUNK
    pair_w = 2 * SSD_HEADDIM
    pairs_per_group = SSD_HEADS // SSD_GROUPS // 2
    off = SSD_HEADS if backward else 0
    edge = 0 if backward else L - 1
    row_i = lax.broadcasted_iota(jnp.int32, (L, L), 0)
    col_i = lax.broadcasted_iota(jnp.int32, (L, L), 1)
    keep = (row_i <= col_i) if backward else (row_i >= col_i)
    left = col_i < SSD_HEADDIM
    tri = jnp.where(keep, 1.0, 0.0).astype(BF16)
    a_dt = dt * a_row
    hi, mid, lo = _split3(a_dt)
    a_cs = _dot(tri, hi) + _dot(tri, mid) + _dot(tri, lo)
    a_cs_t = a_cs.T
    dt_t = dt.T
    c_off = SSD_INNER + SSD_GROUPS * SSD_STATE
    for grp in range(SSD_GROUPS):
        b_g = xbc[:, SSD_INNER + grp * SSD_STATE:SSD_INNER + (grp + 1) * SSD_STATE]
        c_bf = xbc[:, c_off + grp * SSD_STATE:c_off + (grp + 1) * SSD_STATE].astype(BF16)
        cb = _dot_nt(c_bf, b_g.astype(BF16))
        b_t = b_g.T
        states = [state_ref[grp * pairs_per_group + j] for j in range(pairs_per_group)]
        y_off = _dot(c_bf, jnp.concatenate(states, axis=1).astype(BF16))
        for j in range(pairs_per_group):
            pr = grp * pairs_per_group + j
            xs_pair = xbc[:, pr * pair_w:(pr + 1) * pair_w]
            rhs = jnp.concatenate([jnp.where(left, xs_pair, 0.0).astype(BF16),
                                   jnp.where(left, 0.0, xs_pair).astype(BF16)], axis=0)
            m_parts, w_parts, col_parts, tot_parts = [], [], [], []
            for hd in (2 * pr, 2 * pr + 1):
                col = jnp.broadcast_to(a_cs[:, off + hd:off + hd + 1], (L, L))
                row = a_cs_t[off + hd:off + hd + 1, :]
                dt_row = dt_t[off + hd:off + hd + 1, :]
                decay = jnp.exp(jnp.where(keep, col - row, -jnp.inf))
                m_parts.append((cb * decay * dt_row).astype(BF16))
                a_tot = row[:, edge:edge + 1]
                w_parts.append((b_t * (dt_row * jnp.exp(a_tot - row))).astype(BF16))
                col_parts.append(col)
                tot_parts.append(a_tot)
            y = _dot(jnp.concatenate(m_parts, axis=1), rhs)
            y = y + jnp.exp(jnp.where(left, col_parts[0], col_parts[1])) \
                * y_off[:, j * pair_w:(j + 1) * pair_w]
            if d_row is not None:
                y = y + d_row[:, pr * pair_w:(pr + 1) * pair_w] * xs_pair
            y_ref[0, :, pr * pair_w:(pr + 1) * pair_w] = y
            upd = _dot(jnp.concatenate(w_parts, axis=1), rhs)
            keep_frac = jnp.exp(jnp.where(left[0:1], tot_parts[0], tot_parts[1]))
            state_ref[pr] = keep_frac * states[j] + upd


def _ssd_kernel(xf_ref, xb_ref, dtf_ref, dtb_ref, alog_ref, d_ref, yf_ref, yb_ref, sf_ref, sb_ref):
    @pl.when(pl.program_id(1) == 0)
    def _():
        sf_ref[...] = jnp.zeros_like(sf_ref)
        sb_ref[...] = jnp.zeros_like(sb_ref)

    a_row = -jnp.exp(alog_ref[...])
    _ssd_direction(xf_ref[0], dtf_ref[0], a_row, sf_ref, yf_ref, d_ref[...], backward=False)
    _ssd_direction(xb_ref[0], dtb_ref[0], a_row, sb_ref, yb_ref, None, backward=True)


def _ssd(xbc, dt, alog_pad, d_exp):
    b, s, _ = xbc.shape
    nc = s // SSD_CHUNK
    fwd = lambda w: pl.BlockSpec((1, SSD_CHUNK, w), lambda i, c: (i, c, 0))
    bwd = lambda w: pl.BlockSpec((1, SSD_CHUNK, w), lambda i, c: (i, nc - 1 - c, 0))
    y_shape = jax.ShapeDtypeStruct((b, s, SSD_INNER), F32)
    state = pltpu.VMEM((SSD_HEADS // 2, SSD_STATE, 2 * SSD_HEADDIM), F32)
    return pl.pallas_call(
        _ssd_kernel,
        out_shape=(y_shape, y_shape),
        grid=(b, nc),
        in_specs=[fwd(CONV_CH), bwd(CONV_CH), fwd(LANES), bwd(LANES),
                  _const_spec((1, LANES)), _const_spec((1, SSD_INNER))],
        out_specs=(fwd(SSD_INNER), bwd(SSD_INNER)),
        scratch_shapes=[state, state],
        compiler_params=pltpu.CompilerParams(
            dimension_semantics=("arbitrary", "arbitrary"), vmem_limit_bytes=VMEM_LIMIT),
        name="ssd",
    )(xbc, xbc, dt, dt, alog_pad, d_exp)


def _fold_rows(t, op):
    r, c = t.shape
    return op(t.reshape(r // SUBLANES, SUBLANES, c), axis=0)


def _attn_kernel(q_ref, k_ref, v_ref, o_ref, s_scr):
    q_t = q_ref[0, 0]
    tq = q_t.shape[1]
    s_len = k_ref.shape[2]

    def scores(i, m_run):
        start = pl.multiple_of(i * TK_QK, TK_QK)
        s_t = _dot(k_ref[0, 0, pl.ds(start, TK_QK), :], q_t)
        s_scr[pl.ds(start, TK_QK), :] = s_t
        return jnp.maximum(m_run, _fold_rows(s_t, jnp.max))

    m_run = lax.fori_loop(0, s_len // TK_QK, scores, jnp.full((SUBLANES, tq), -jnp.inf, F32))
    m = jnp.max(m_run, axis=0, keepdims=True)

    def values(i, carry):
        l_run, acc = carry
        start = pl.multiple_of(i * TK_PV, TK_PV)
        p_t = jnp.exp2(s_scr[pl.ds(start, TK_PV), :] - m)
        l_run = l_run + _fold_rows(p_t, jnp.sum)
        acc = acc + _dot(v_ref[0, 0, :, pl.ds(start, TK_PV)], p_t.astype(BF16))
        return l_run, acc

    l_run, acc = lax.fori_loop(
        0, s_len // TK_PV, values,
        (jnp.zeros((SUBLANES, tq), F32), jnp.zeros((V_DIM, tq), F32)))
    l = jnp.sum(l_run, axis=0, keepdims=True)
    o_ref[0] = (acc / l).T


def _attention(q_t, k, v_t):
    b, nh, _, s = q_t.shape
    return pl.pallas_call(
        _attn_kernel,
        out_shape=jax.ShapeDtypeStruct((b, s, nh * V_DIM), F32),
        grid=(b, nh, s // TQ),
        in_specs=[
            pl.BlockSpec((1, 1, HEAD_PAD, TQ), lambda i, h, j: (i, h, 0, j)),
            pl.BlockSpec((1, 1, s, HEAD_PAD), lambda i, h, j: (i, h, 0, 0)),
            pl.BlockSpec((1, 1, V_DIM, s), lambda i, h, j: (i, h, 0, 0)),
        ],
        out_specs=pl.BlockSpec((1, TQ, V_DIM), lambda i, h, j: (i, j, h)),
        scratch_shapes=[pltpu.VMEM((s, TQ), F32)],
        compiler_params=pltpu.CompilerParams(
            dimension_semantics=("arbitrary", "arbitrary", "arbitrary"),
            vmem_limit_bytes=VMEM_LIMIT),
        name="attention",
    )(q_t, k, v_t)


def _mix_out_kernel(x_ref, mod_ref, yf_ref, yb_ref, z_ref, at_ref, sg_ref, ag_ref,
                    wo_s_ref, wo_a_ref, o_ref):
    gate = mod_ref[0, 5:6, :]
    z = z_ref[0]
    y = (yf_ref[0] + yb_ref[0]) * (z * _sigmoid(z))
    gw = SSD_INNER // SSD_GROUPS
    parts = [_rms(y[:, i * gw:(i + 1) * gw], sg_ref[:, i * gw:(i + 1) * gw]).astype(BF16)
             for i in range(SSD_GROUPS)]
    ssd_out = jnp.concatenate(parts, axis=1)
    attn = _rms(at_ref[0], ag_ref[...]).astype(BF16)
    mixed = _dot(ssd_out, wo_s_ref[...]) + _dot(attn, wo_a_ref[...])
    o_ref[0] = x_ref[0] + gate * mixed


def _mix_out(x, mod, y_f, y_b, z, attn, sg, ag, wo_s, wo_a):
    b, s, d = x.shape
    tile = pl.BlockSpec((1, TM_OUT, d), lambda i, j: (i, j, 0))
    return pl.pallas_call(
        _mix_out_kernel,
        out_shape=jax.ShapeDtypeStruct(x.shape, F32),
        grid=(b, s // TM_OUT),
        in_specs=[tile, pl.BlockSpec((1, N_MOD, d), lambda i, j: (i, 0, 0)),
                  tile, tile, tile, tile,
                  _const_spec((1, d)), _const_spec((1, d)),
                  _const_spec(wo_s.shape), _const_spec(wo_a.shape)],
        out_specs=tile,
        compiler_params=pltpu.CompilerParams(
            dimension_semantics=("arbitrary", "arbitrary"), vmem_limit_bytes=VMEM_LIMIT),
        name="mix_out",
    )(x, mod, y_f, y_b, z, attn, sg, ag, wo_s, wo_a)


def _pad_cols(w, width):
    return jnp.pad(w, ((0, 0), (0, width - w.shape[1])))


def _prep_w_in(w_in):
    cuts = np.cumsum([SSD_INNER, CONV_CH, 2 * SSD_HEADS, Q_LORA, KV_LORA])
    w_z, w_xbc, w_dt, w_q, w_kv, w_kr = jnp.split(w_in, [int(c) for c in cuts], axis=1)
    t1, t2 = w_kr[:, :QK_ROPE // 2], w_kr[:, QK_ROPE // 2:]
    kra = _pad_cols(jnp.concatenate([t1, t2], axis=1), LANES)
    krb = _pad_cols(jnp.concatenate([-t2, t1], axis=1), LANES)
    return jnp.concatenate([w_z, w_xbc, _pad_cols(w_dt, LANES), w_q, w_kv, kra, krb],
                           axis=1).astype(BF16)


def _prep_w_q(w_q_b):
    w = w_q_b.reshape(Q_LORA, MLA_HEADS, QK_NOPE + QK_ROPE)
    nope = w[:, :, :QK_NOPE]
    t1 = w[:, :, QK_NOPE:QK_NOPE + QK_ROPE // 2]
    t2 = w[:, :, QK_NOPE + QK_ROPE // 2:]
    zero = jnp.zeros((Q_LORA, MLA_HEADS, LANES - QK_ROPE), w.dtype)
    ra = jnp.concatenate([t1, t2, zero], axis=2)
    rb = jnp.concatenate([-t2, t1, zero], axis=2)
    flat = lambda t: t.reshape(Q_LORA, MLA_HEADS * LANES)
    return jnp.concatenate([flat(nope), flat(ra), flat(rb)], axis=1).T.astype(BF16)


def _prep_w_kv(w_kv_b):
    w = w_kv_b.reshape(KV_LORA, MLA_HEADS, QK_NOPE + V_DIM)
    w_kn = w[:, :, :QK_NOPE].reshape(KV_LORA, MLA_HEADS * QK_NOPE).astype(BF16)
    w_v_t = w[:, :, QK_NOPE:].reshape(KV_LORA, MLA_HEADS * V_DIM).T.astype(BF16)
    return w_kn, w_v_t


def kernel(x, c, positions, ada_w, ada_b, norm_g, ffn_w_gate, ffn_w_up, ffn_w_down, w_in, conv_w,
           conv_b, dt_bias, a_log, d_skip, ssd_norm_g, q_norm_g, w_q_b, kv_norm_g, w_kv_b,
           attn_norm_g, w_out, final_norm_g):
    b, s, d = x.shape
    depth = ada_w.shape[0]
    assert s % TM_FFN == 0 and s % TM_MIX == 0 and s % TM_OUT == 0 and s % SSD_CHUNK == 0
    assert s % TQ == 0 and s % TK_QK == 0 and s % TK_PV == 0 and s % ROPE_TS == 0

    c_pad = jnp.pad(c, ((0, SUBLANES - b), (0, 0)))
    mod_all = _ada(c_pad, ada_w, ada_b)
    tabs = _rope_tables(positions)
    fg = final_norm_g.reshape(1, d)

    for l in range(depth):
        mod = mod_all[l, :b].reshape(b, N_MOD, d)
        bf = lambda w: w.astype(BF16)
        x = _ffn(x, mod, norm_g[l, 0].reshape(1, d), bf(ffn_w_gate[l, 0]), bf(ffn_w_up[l, 0]),
                 bf(ffn_w_down[l, 0]), fg, row0=0, final=False)

        w_kn, w_v_t = _prep_w_kv(w_kv_b[l])
        dtb = _pad_cols(dt_bias[l].reshape(1, 2 * SSD_HEADS), LANES)
        z, xbc, dt, q_t, k, v_t = _mix_in(
            x, mod, norm_g[l, 1].reshape(1, d), _prep_w_in(w_in[l]), conv_w[l],
            conv_b[l].reshape(1, CONV_CH), dtb, q_norm_g[l].reshape(1, Q_LORA), _prep_w_q(w_q_b[l]),
            kv_norm_g[l].reshape(1, KV_LORA), w_kn, w_v_t, tabs)

        alog_pad = _pad_cols(a_log[l].reshape(1, 2 * SSD_HEADS), LANES)
        d_exp = jnp.repeat(d_skip[l], SSD_HEADDIM).reshape(1, SSD_INNER)
        y_f, y_b = _ssd(xbc, dt, alog_pad, d_exp)
        attn = _attention(q_t, k, v_t)

        x = _mix_out(x, mod, y_f, y_b, z, attn, ssd_norm_g[l].reshape(1, SSD_INNER),
                     attn_norm_g[l].reshape(1, MLA_HEADS * V_DIM),
                     bf(w_out[l, :SSD_INNER]), bf(w_out[l, SSD_INNER:]))

        x = _ffn(x, mod, norm_g[l, 2].reshape(1, d), bf(ffn_w_gate[l, 1]), bf(ffn_w_up[l, 1]),
                 bf(ffn_w_down[l, 1]), fg, row0=6, final=(l == depth - 1))
    return x
```

```python
import functools
import math

import jax
import jax.numpy as jnp
import numpy as np
from jax import lax
from jax.experimental import pallas as pl
from jax.experimental.pallas import tpu as pltpu

F32 = jnp.float32
BF16 = jnp.bfloat16

D_MODEL = 1024
D_FF = 2816
N_MOD = 9
SSD_INNER = 1024
SSD_HEADDIM = 64
SSD_HEADS = 16
SSD_GROUPS = 2
SSD_STATE = 128
SSD_CONV = 5
SSD_CHUNK = 128
CONV_CH = SSD_INNER + 2 * SSD_GROUPS * SSD_STATE
MLA_HEADS = 8
QK_NOPE = 128
QK_ROPE = 64
V_DIM = 128
V_EXT = V_DIM + 16
Q_LORA = 512
KV_LORA = 256
ROPE_THETA = 10000.0
EPS = 1e-6

LANES = 128
SUBLANES = 8
HEAD_PAD = 256
VMEM_LIMIT = 56 * 1024 * 1024

C_Z = 0
C_XBC = C_Z + SSD_INNER
C_DT = C_XBC + CONV_CH
C_Q = C_DT + LANES
C_KV = C_Q + Q_LORA
C_KRA = C_KV + KV_LORA
C_KRB = C_KRA + LANES
IN_COLS_EXT = C_KRB + LANES

TM_FFN = 512
TM_MIX = 256
TM_OUT = 512
TQ = 512
TK_ATTN = 2048
ROPE_TS = 1024
CONV_HALO = SUBLANES
Q_SCALE = (QK_NOPE + QK_ROPE) ** -0.5 * math.log2(math.e)


def _dot(a, b):
    return jnp.dot(a, b, preferred_element_type=F32)


def _dot_nt(a, b):
    return lax.dot_general(a, b, (((1,), (1,)), ((), ())), preferred_element_type=F32)


def _rms(x, g):
    return x * lax.rsqrt(jnp.mean(x * x, axis=-1, keepdims=True) + EPS) * g


def _modulate(x, g, shift, scale):
    return _rms(x, g) * (1.0 + scale) + shift


def _sigmoid(x):
    return 1.0 / (1.0 + jnp.exp(-x))


def _const_spec(shape):
    nd = len(shape)
    return pl.BlockSpec(shape, lambda *_: (0,) * nd, pipeline_mode=pl.Buffered(1))


def _ada_kernel(c_ref, w_ref, b_ref, o_ref):
    c = c_ref[...]
    c_act = (c * _sigmoid(c)).astype(BF16)
    o_ref[0] = _dot(c_act, w_ref[0].astype(BF16)) + b_ref[0]


def _ada(c_pad, ada_w, ada_b):
    depth, d, n = ada_w.shape
    tn = 1536
    return pl.pallas_call(
        _ada_kernel,
        out_shape=jax.ShapeDtypeStruct((depth, SUBLANES, n), F32),
        grid=(depth, n // tn),
        in_specs=[
            pl.BlockSpec((SUBLANES, d), lambda l, j: (0, 0)),
            pl.BlockSpec((1, d, tn), lambda l, j: (l, 0, j)),
            pl.BlockSpec((1, 1, tn), lambda l, j: (l, 0, j)),
        ],
        out_specs=pl.BlockSpec((1, SUBLANES, tn), lambda l, j: (l, 0, j)),
        compiler_params=pltpu.CompilerParams(
            dimension_semantics=("arbitrary", "arbitrary"), vmem_limit_bytes=VMEM_LIMIT),
        name="ada",
    )(c_pad, ada_w, ada_b.reshape(depth, 1, n))


def _rope_kernel(pos_ref, inv_ref, cos_t_ref, sin_t_ref, cos_n_ref, sin_n_ref):
    ang = inv_ref[...] * pos_ref[0].astype(F32)
    c, s = jnp.cos(ang), jnp.sin(ang)
    zero = jnp.zeros((LANES - QK_ROPE, ang.shape[1]), F32)
    c_t = jnp.concatenate([c, c, zero], axis=0)
    s_t = jnp.concatenate([s, s, zero], axis=0)
    cos_t_ref[0] = c_t
    sin_t_ref[0] = s_t
    cos_n_ref[0] = c_t.T
    sin_n_ref[0] = s_t.T


def _rope_tables(positions):
    b, s = positions.shape
    inv_freq = 1.0 / (ROPE_THETA ** (jnp.arange(0, QK_ROPE, 2, dtype=F32) / QK_ROPE))
    t_shape = jax.ShapeDtypeStruct((b, LANES, s), F32)
    n_shape = jax.ShapeDtypeStruct((b, s, LANES), F32)
    t_spec = pl.BlockSpec((1, LANES, ROPE_TS), lambda i, j: (i, 0, j))
    n_spec = pl.BlockSpec((1, ROPE_TS, LANES), lambda i, j: (i, j, 0))
    return pl.pallas_call(
        _rope_kernel,
        out_shape=(t_shape, t_shape, n_shape, n_shape),
        grid=(b, s // ROPE_TS),
        in_specs=[
            pl.BlockSpec((1, 1, ROPE_TS), lambda i, j: (i, 0, j)),
            pl.BlockSpec((QK_ROPE // 2, 1), lambda i, j: (0, 0)),
        ],
        out_specs=(t_spec, t_spec, n_spec, n_spec),
        compiler_params=pltpu.CompilerParams(dimension_semantics=("arbitrary", "arbitrary")),
        name="rope_tables",
    )(positions.reshape(b, 1, s), inv_freq.reshape(QK_ROPE // 2, 1))


def _ffn_kernel(x_ref, mod_ref, g_ref, wg_ref, wu_ref, wd_ref, fg_ref, o_ref, *, row0, final):
    x = x_ref[0]
    shift = mod_ref[0, row0:row0 + 1, :]
    scale = mod_ref[0, row0 + 1:row0 + 2, :]
    gate = mod_ref[0, row0 + 2:row0 + 3, :]
    h = _modulate(x, g_ref[...], shift, scale).astype(BF16)
    a = _dot(h, wg_ref[...])
    u = _dot(h, wu_ref[...])
    act = (a * _sigmoid(a) * u).astype(BF16)
    out = x + 0.5 * gate * _dot(act, wd_ref[...])
    if final:
        out = _rms(out, fg_ref[...])
    o_ref[0] = out


def _ffn(x, mod, g, wg, wu, wd, fg, *, row0, final):
    b, s, d = x.shape
    tile = pl.BlockSpec((1, TM_FFN, d), lambda i, j: (i, j, 0))
    return pl.pallas_call(
        functools.partial(_ffn_kernel, row0=row0, final=final),
        out_shape=jax.ShapeDtypeStruct(x.shape, F32),
        grid=(b, s // TM_FFN),
        in_specs=[
            tile,
            pl.BlockSpec((1, N_MOD, d), lambda i, j: (i, 0, 0)),
            _const_spec((1, d)),
            _const_spec(wg.shape),
            _const_spec(wu.shape),
            _const_spec(wd.shape),
            _const_spec((1, d)),
        ],
        out_specs=tile,
        compiler_params=pltpu.CompilerParams(
            dimension_semantics=("arbitrary", "arbitrary"), vmem_limit_bytes=VMEM_LIMIT),
        name="ffn",
    )(x, mod, g, wg, wu, wd, fg)


def _mix_in_kernel(x_ref, xp_ref, xn_ref, mod_ref, g_ref, win_ref, cw_ref, cb_ref, dtb_ref,
                   qg_ref, wq_ref, kvg_ref, wkn_ref, wv_ref,
                   cos_t_ref, sin_t_ref, cos_n_ref, sin_n_ref,
                   z_ref, xbc_ref, dt_ref, q_ref, k_ref, v_ref, u_scr):
    j = pl.program_id(1)
    nj = pl.num_programs(1)
    tm = x_ref.shape[1]
    g = g_ref[...]
    shift = mod_ref[0, 3:4, :]
    scale = mod_ref[0, 4:5, :]

    h = _modulate(x_ref[0], g, shift, scale).astype(BF16)
    proj = _dot(h, win_ref[...])

    z_ref[0] = proj[:, C_Z:C_XBC].astype(BF16)
    dt_ref[0] = jax.nn.softplus(proj[:, C_DT:C_Q] + dtb_ref[...])

    x_halo = jnp.concatenate([xp_ref[0], xn_ref[0]], axis=0)
    h_halo = _modulate(x_halo, g, shift, scale).astype(BF16)
    u_halo = _dot(h_halo, win_ref[:, C_XBC:C_DT])
    u_scr[0:CONV_HALO, :] = jnp.where(j > 0, u_halo[:CONV_HALO], 0.0)
    u_scr[CONV_HALO:CONV_HALO + tm, :] = proj[:, C_XBC:C_DT]
    u_scr[CONV_HALO + tm:, :] = jnp.where(j < nj - 1, u_halo[CONV_HALO:], 0.0)
    acc = jnp.broadcast_to(cb_ref[...], (tm, CONV_CH))
    for k in range(SSD_CONV):
        off = CONV_HALO - SSD_CONV // 2 + k
        acc = acc + cw_ref[k:k + 1, :] * u_scr[off:off + tm, :]
    xbc_ref[0] = acc * _sigmoid(acc)

    hq = _rms(proj[:, C_Q:C_KV], qg_ref[...]).astype(BF16)
    q_all = _dot_nt(wq_ref[...], hq)
    ckv = _rms(proj[:, C_KV:C_KRA], kvg_ref[...]).astype(BF16)
    k_nope = _dot(ckv, wkn_ref[...])
    v_all = _dot_nt(wv_ref[...], ckv)
    k_rope = (proj[:, C_KRA:C_KRB] * cos_n_ref[0] + proj[:, C_KRB:] * sin_n_ref[0]).astype(BF16)
    half = QK_ROPE // 2
    cos_t = cos_t_ref[0, 0:half, :] * Q_SCALE
    sin_t = sin_t_ref[0, 0:half, :] * Q_SCALE
    n_rows = MLA_HEADS * LANES
    for hd in range(MLA_HEADS):
        r = slice(hd * LANES, (hd + 1) * LANES)
        q_ref[0, hd, 0:LANES, :] = (q_all[r] * Q_SCALE).astype(BF16)
        t1 = q_all[n_rows + hd * QK_ROPE:n_rows + hd * QK_ROPE + half]
        t2 = q_all[n_rows + hd * QK_ROPE + half:n_rows + (hd + 1) * QK_ROPE]
        q_ref[0, hd, LANES:LANES + half, :] = (t1 * cos_t - t2 * sin_t).astype(BF16)
        q_ref[0, hd, LANES + half:LANES + QK_ROPE, :] = (t2 * cos_t + t1 * sin_t).astype(BF16)
        q_ref[0, hd, LANES + QK_ROPE:, :] = jnp.zeros((HEAD_PAD - LANES - QK_ROPE, tm), BF16)
        k_ref[0, hd, :, 0:LANES] = k_nope[:, r].astype(BF16)
        k_ref[0, hd, :, LANES:] = k_rope
        v_ref[0, hd, 0:V_DIM, :] = v_all[r].astype(BF16)
        v_ref[0, hd, V_DIM:, :] = jnp.ones((V_EXT - V_DIM, tm), BF16)


def _mix_in(x, mod, g, win, cw, cb, dtb, qg, wq, kvg, wkn, wv, tabs):
    b, s, d = x.shape
    tm = TM_MIX
    cos_t, sin_t, cos_n, sin_n = tabs
    nblk = s // CONV_HALO
    per = tm // CONV_HALO
    out_shape = (
        jax.ShapeDtypeStruct((b, s, SSD_INNER), BF16),
        jax.ShapeDtypeStruct((b, s, CONV_CH), F32),
        jax.ShapeDtypeStruct((b, s, LANES), F32),
        jax.ShapeDtypeStruct((b, MLA_HEADS, HEAD_PAD, s), BF16),
        jax.ShapeDtypeStruct((b, MLA_HEADS, s, HEAD_PAD), BF16),
        jax.ShapeDtypeStruct((b, MLA_HEADS, V_EXT, s), BF16),
    )
    tok = lambda w: pl.BlockSpec((1, tm, w), lambda i, j: (i, j, 0))
    in_specs = [
        tok(d),
        pl.BlockSpec((1, CONV_HALO, d), lambda i, j: (i, jnp.maximum(j * per - 1, 0), 0)),
        pl.BlockSpec((1, CONV_HALO, d), lambda i, j: (i, jnp.minimum((j + 1) * per, nblk - 1), 0)),
        pl.BlockSpec((1, N_MOD, d), lambda i, j: (i, 0, 0)),
        _const_spec((1, d)),
        _const_spec(win.shape),
        _const_spec(cw.shape),
        _const_spec(cb.shape),
        _const_spec(dtb.shape),
        _const_spec(qg.shape),
        _const_spec(wq.shape),
        _const_spec(kvg.shape),
        _const_spec(wkn.shape),
        _const_spec(wv.shape),
        pl.BlockSpec((1, LANES, tm), lambda i, j: (i, 0, j)),
        pl.BlockSpec((1, LANES, tm), lambda i, j: (i, 0, j)),
        tok(LANES),
        tok(LANES),
    ]
    out_specs = (
        tok(SSD_INNER),
        tok(CONV_CH),
        tok(LANES),
        pl.BlockSpec((1, MLA_HEADS, HEAD_PAD, tm), lambda i, j: (i, 0, 0, j)),
        pl.BlockSpec((1, MLA_HEADS, tm, HEAD_PAD), lambda i, j: (i, 0, j, 0)),
        pl.BlockSpec((1, MLA_HEADS, V_EXT, tm), lambda i, j: (i, 0, 0, j)),
    )
    return pl.pallas_call(
        _mix_in_kernel,
        out_shape=out_shape,
        grid=(b, s // tm),
        in_specs=in_specs,
        out_specs=out_specs,
        scratch_shapes=[pltpu.VMEM((tm + 2 * CONV_HALO, CONV_CH), F32)],
        compiler_params=pltpu.CompilerParams(
            dimension_semantics=("arbitrary", "arbitrary"), vmem_limit_bytes=VMEM_LIMIT),
        name="mix_in",
    )(x, x, x, mod, g, win, cw, cb, dtb, qg, wq, kvg, wkn, wv, cos_t, sin_t, cos_n, sin_n)


def _split3(x):
    hi = x.astype(BF16)
    r1 = x - hi.astype(F32)
    mid = r1.astype(BF16)
    lo = (r1 - mid.astype(F32)).astype(BF16)
    return hi, mid, lo


def _ssd_direction(xbc, dt, a_row, state_ref, y_ref, d_row, *, backward):
    L = SSD_CHUNK
    pair_w = 2 * SSD_HEADDIM
    pairs_per_group = SSD_HEADS // SSD_GROUPS // 2
    off = SSD_HEADS if backward else 0
    edge = 0 if backward else L - 1
    row_i = lax.broadcasted_iota(jnp.int32, (L, L), 0)
    col_i = lax.broadcasted_iota(jnp.int32, (L, L), 1)
    keep = (row_i <= col_i) if backward else (row_i >= col_i)
    left = col_i < SSD_HEADDIM
    tri = jnp.where(keep, 1.0, 0.0).astype(BF16)
    a_dt = dt * a_row
    hi, mid, lo = _split3(a_dt)
    a_cs = _dot(tri, hi) + _dot(tri, mid) + _dot(tri, lo)
    a_cs_t = a_cs.T
    dt_t = dt.T
    c_off = SSD_INNER + SSD_GROUPS * SSD_STATE
    for grp in range(SSD_GROUPS):
        b_g = xbc[:, SSD_INNER + grp * SSD_STATE:SSD_INNER + (grp + 1) * SSD_STATE]
        c_bf = xbc[:, c_off + grp * SSD_STATE:c_off + (grp + 1) * SSD_STATE].astype(BF16)
        cb = _dot_nt(c_bf, b_g.astype(BF16))
        b_t = b_g.T
        states = [state_ref[grp * pairs_per_group + j] for j in range(pairs_per_group)]
        y_off = _dot(c_bf, jnp.concatenate(states, axis=1).astype(BF16))
        for j in range(pairs_per_group):
            pr = grp * pairs_per_group + j
            xs_pair = xbc[:, pr * pair_w:(pr + 1) * pair_w]
            rhs = jnp.concatenate([jnp.where(left, xs_pair, 0.0).astype(BF16),
                                   jnp.where(left, 0.0, xs_pair).astype(BF16)], axis=0)
            m_parts, w_parts, col_parts, tot_parts = [], [], [], []
            for hd in (2 * pr, 2 * pr + 1):
                col = jnp.broadcast_to(a_cs[:, off + hd:off + hd + 1], (L, L))
                row = a_cs_t[off + hd:off + hd + 1, :]
                dt_row = dt_t[off + hd:off + hd + 1, :]
                decay = jnp.exp(jnp.where(keep, col - row, -jnp.inf))
                m_parts.append((cb * decay * dt_row).astype(BF16))
                a_tot = row[:, edge:edge + 1]
                w_parts.append((b_t * (dt_row * jnp.exp(a_tot - row))).astype(BF16))
                col_parts.append(col)
                tot_parts.append(a_tot)
            y = _dot(jnp.concatenate(m_parts, axis=1), rhs)
            y = y + jnp.exp(jnp.where(left, col_parts[0], col_parts[1])) \
                * y_off[:, j * pair_w:(j + 1) * pair_w]
            if d_row is not None:
                y = y + d_row[:, pr * pair_w:(pr + 1) * pair_w] * xs_pair
            y_ref[0, :, pr * pair_w:(pr + 1) * pair_w] = y.astype(BF16)
            upd = _dot(jnp.concatenate(w_parts, axis=1), rhs)
            keep_frac = jnp.exp(jnp.where(left[0:1], tot_parts[0], tot_parts[1]))
            state_ref[pr] = keep_frac * states[j] + upd


def _ssd_kernel(xf_ref, xb_ref, dtf_ref, dtb_ref, alog_ref, d_ref, yf_ref, yb_ref, sf_ref, sb_ref):
    @pl.when(pl.program_id(1) == 0)
    def _():
        sf_ref[...] = jnp.zeros_like(sf_ref)
        sb_ref[...] = jnp.zeros_like(sb_ref)

    a_row = -jnp.exp(alog_ref[...])
    _ssd_direction(xf_ref[0], dtf_ref[0], a_row, sf_ref, yf_ref, d_ref[...], backward=False)
    _ssd_direction(xb_ref[0], dtb_ref[0], a_row, sb_ref, yb_ref, None, backward=True)


def _ssd(xbc, dt, alog_pad, d_exp):
    b, s, _ = xbc.shape
    nc = s // SSD_CHUNK
    fwd = lambda w: pl.BlockSpec((1, SSD_CHUNK, w), lambda i, c: (i, c, 0))
    bwd = lambda w: pl.BlockSpec((1, SSD_CHUNK, w), lambda i, c: (i, nc - 1 - c, 0))
    y_shape = jax.ShapeDtypeStruct((b, s, SSD_INNER), BF16)
    state = pltpu.VMEM((SSD_HEADS // 2, SSD_STATE, 2 * SSD_HEADDIM), F32)
    return pl.pallas_call(
        _ssd_kernel,
        out_shape=(y_shape, y_shape),
        grid=(b, nc),
        in_specs=[fwd(CONV_CH), bwd(CONV_CH), fwd(LANES), bwd(LANES),
                  _const_spec((1, LANES)), _const_spec((1, SSD_INNER))],
        out_specs=(fwd(SSD_INNER), bwd(SSD_INNER)),
        scratch_shapes=[state, state],
        compiler_params=pltpu.CompilerParams(
            dimension_semantics=("arbitrary", "arbitrary"), vmem_limit_bytes=VMEM_LIMIT),
        name="ssd",
    )(xbc, xbc, dt, dt, alog_pad, d_exp)


def _fold_rows(t, op):
    r, c = t.shape
    return op(t.reshape(r // SUBLANES, SUBLANES, c), axis=0)


def _attn_kernel(q_ref, k_ref, v_ref, o_ref, sa_scr, sb_scr, ma_scr, mb_scr):
    t = pl.program_id(2)
    n_q = pl.num_programs(2) - 1
    q_t = q_ref[0, 0]
    tq = q_t.shape[1]
    n_steps = k_ref.shape[2] // TK_ATTN

    def scores(s_new, i, m_run):
        start = pl.multiple_of(i * TK_ATTN, TK_ATTN)
        s_t = _dot(k_ref[0, 0, pl.ds(start, TK_ATTN), :], q_t)
        s_new[pl.ds(start, TK_ATTN), :] = s_t
        return jnp.maximum(m_run, _fold_rows(s_t, jnp.max))

    def values(s_old, i, m, acc):
        start = pl.multiple_of(i * TK_ATTN, TK_ATTN)
        p_t = jnp.exp2(s_old[pl.ds(start, TK_ATTN), :] - m)
        return acc + _dot(v_ref[0, 0, :, pl.ds(start, TK_ATTN)], p_t.astype(BF16))

    m_init = jnp.full((SUBLANES, tq), -jnp.inf, F32)
    acc_init = jnp.zeros((V_EXT, tq), F32)

    def finish_values(acc):
        o_ref[0] = (acc[:V_DIM] / acc[V_DIM:V_DIM + 1]).T.astype(BF16)

    def stage(s_new, m_new, s_old, m_old):
        @pl.when(t == 0)
        def _():
            m_new[...] = lax.fori_loop(0, n_steps, functools.partial(scores, s_new), m_init)

        @pl.when(jnp.logical_and(t > 0, t < n_q))
        def _():
            m = jnp.max(m_old[...], axis=0, keepdims=True)

            def both(i, carry):
                m_run, acc = carry
                return scores(s_new, i, m_run), values(s_old, i, m, acc)

            m_run, acc = lax.fori_loop(0, n_steps, both, (m_init, acc_init))
            m_new[...] = m_run
            finish_values(acc)

        @pl.when(t == n_q)
        def _():
            m = jnp.max(m_old[...], axis=0, keepdims=True)
            finish_values(lax.fori_loop(
                0, n_steps, lambda i, acc: values(s_old, i, m, acc), acc_init))

    even = lax.rem(t, 2) == 0
    pl.when(even)(lambda: stage(sa_scr, ma_scr, sb_scr, mb_scr))
    pl.when(jnp.logical_not(even))(lambda: stage(sb_scr, mb_scr, sa_scr, ma_scr))


def _attention(q_t, k, v_t):
    b, nh, _, s = q_t.shape
    n_q = s // TQ
    return pl.pallas_call(
        _attn_kernel,
        out_shape=jax.ShapeDtypeStruct((b, s, nh * V_DIM), BF16),
        grid=(b, nh, n_q + 1),
        in_specs=[
            pl.BlockSpec((1, 1, HEAD_PAD, TQ), lambda i, h, j: (i, h, 0, jnp.minimum(j, n_q - 1))),
            pl.BlockSpec((1, 1, s, HEAD_PAD), lambda i, h, j: (i, h, 0, 0)),
            pl.BlockSpec((1, 1, V_EXT, s), lambda i, h, j: (i, h, 0, 0)),
        ],
        out_specs=pl.BlockSpec((1, TQ, V_DIM), lambda i, h, j: (i, jnp.maximum(j - 1, 0), h)),
        scratch_shapes=[pltpu.VMEM((s, TQ), F32), pltpu.VMEM((s, TQ), F32),
                        pltpu.VMEM((SUBLANES, TQ), F32), pltpu.VMEM((SUBLANES, TQ), F32)],
        compiler_params=pltpu.CompilerParams(
            dimension_semantics=("arbitrary", "arbitrary", "arbitrary"),
            vmem_limit_bytes=VMEM_LIMIT),
        name="attention",
    )(q_t, k, v_t)


def _mix_out_kernel(x_ref, mod_ref, yf_ref, yb_ref, z_ref, at_ref, sg_ref, ag_ref,
                    wo_s_ref, wo_a_ref, o_ref):
    gate = mod_ref[0, 5:6, :]
    z = z_ref[0].astype(F32)
    y = (yf_ref[0].astype(F32) + yb_ref[0].astype(F32)) * (z * _sigmoid(z))
    gw = SSD_INNER // SSD_GROUPS
    parts = [_rms(y[:, i * gw:(i + 1) * gw], sg_ref[:, i * gw:(i + 1) * gw]).astype(BF16)
             for i in range(SSD_GROUPS)]
    ssd_out = jnp.concatenate(parts, axis=1)
    attn = _rms(at_ref[0].astype(F32), ag_ref[...]).astype(BF16)
    mixed = _dot(ssd_out, wo_s_ref[...]) + _dot(attn, wo_a_ref[...])
    o_ref[0] = x_ref[0] + gate * mixed


def _mix_out(x, mod, y_f, y_b, z, attn, sg, ag, wo_s, wo_a):
    b, s, d = x.shape
    tile = pl.BlockSpec((1, TM_OUT, d), lambda i, j: (i, j, 0))
    return pl.pallas_call(
        _mix_out_kernel,
        out_shape=jax.ShapeDtypeStruct(x.shape, F32),
        grid=(b, s // TM_OUT),
        in_specs=[tile, pl.BlockSpec((1, N_MOD, d), lambda i, j: (i, 0, 0)),
                  tile, tile, tile, tile,
                  _const_spec((1, d)), _const_spec((1, d)),
                  _const_spec(wo_s.shape), _const_spec(wo_a.shape)],
        out_specs=tile,
        compiler_params=pltpu.CompilerParams(
            dimension_semantics=("arbitrary", "arbitrary"), vmem_limit_bytes=VMEM_LIMIT),
        name="mix_out",
    )(x, mod, y_f, y_b, z, attn, sg, ag, wo_s, wo_a)


def _pad_cols(w, width):
    return jnp.pad(w, ((0, 0), (0, width - w.shape[1])))


def _prep_w_in(w_in):
    cuts = np.cumsum([SSD_INNER, CONV_CH, 2 * SSD_HEADS, Q_LORA, KV_LORA])
    w_z, w_xbc, w_dt, w_q, w_kv, w_kr = jnp.split(w_in, [int(c) for c in cuts], axis=1)
    t1, t2 = w_kr[:, :QK_ROPE // 2], w_kr[:, QK_ROPE // 2:]
    kra = _pad_cols(jnp.concatenate([t1, t2], axis=1), LANES)
    krb = _pad_cols(jnp.concatenate([-t2, t1], axis=1), LANES)
    return jnp.concatenate([w_z, w_xbc, _pad_cols(w_dt, LANES), w_q, w_kv, kra, krb],
                           axis=1).astype(BF16)


def _prep_w_q(w_q_b):
    w = w_q_b.reshape(Q_LORA, MLA_HEADS, QK_NOPE + QK_ROPE)
    nope = w[:, :, :QK_NOPE].reshape(Q_LORA, MLA_HEADS * QK_NOPE)
    rope = w[:, :, QK_NOPE:].reshape(Q_LORA, MLA_HEADS * QK_ROPE)
    return jnp.concatenate([nope, rope], axis=1).T.astype(BF16)


def _prep_w_kv(w_kv_b):
    w = w_kv_b.reshape(KV_LORA, MLA_HEADS, QK_NOPE + V_DIM)
    w_kn = w[:, :, :QK_NOPE].reshape(KV_LORA, MLA_HEADS * QK_NOPE).astype(BF16)
    w_v_t = w[:, :, QK_NOPE:].reshape(KV_LORA, MLA_HEADS * V_DIM).T.astype(BF16)
    return w_kn, w_v_t


def kernel(x, c, positions, ada_w, ada_b, norm_g, ffn_w_gate, ffn_w_up, ffn_w_down, w_in, conv_w,
           conv_b, dt_bias, a_log, d_skip, ssd_norm_g, q_norm_g, w_q_b, kv_norm_g, w_kv_b,
           attn_norm_g, w_out, final_norm_g):
    b, s, d = x.shape
    depth = ada_w.shape[0]
    assert s % TM_FFN == 0 and s % TM_MIX == 0 and s % TM_OUT == 0 and s % SSD_CHUNK == 0
    assert s % TQ == 0 and s % TK_ATTN == 0 and s % ROPE_TS == 0

    c_pad = jnp.pad(c, ((0, SUBLANES - b), (0, 0)))
    mod_all = _ada(c_pad, ada_w, ada_b)
    tabs = _rope_tables(positions)
    fg = final_norm_g.reshape(1, d)

    for l in range(depth):
        mod = mod_all[l, :b].reshape(b, N_MOD, d)
        bf = lambda w: w.astype(BF16)
        x = _ffn(x, mod, norm_g[l, 0].reshape(1, d), bf(ffn_w_gate[l, 0]), bf(ffn_w_up[l, 0]),
                 bf(ffn_w_down[l, 0]), fg, row0=0, final=False)

        w_kn, w_v_t = _prep_w_kv(w_kv_b[l])
        dtb = _pad_cols(dt_bias[l].reshape(1, 2 * SSD_HEADS), LANES)
        z, xbc, dt, q_t, k, v_t = _mix_in(
            x, mod, norm_g[l, 1].reshape(1, d), _prep_w_in(w_in[l]), conv_w[l],
            conv_b[l].reshape(1, CONV_CH), dtb, q_norm_g[l].reshape(1, Q_LORA), _prep_w_q(w_q_b[l]),
            kv_norm_g[l].reshape(1, KV_LORA), w_kn, w_v_t, tabs)

        alog_pad = _pad_cols(a_log[l].reshape(1, 2 * SSD_HEADS), LANES)
        d_exp = jnp.repeat(d_skip[l], SSD_HEADDIM).reshape(1, SSD_INNER)
        y_f, y_b = _ssd(xbc, dt, alog_pad, d_exp)
        attn = _attention(q_t, k, v_t)

        x = _mix_out(x, mod, y_f, y_b, z, attn, ssd_norm_g[l].reshape(1, SSD_INNER),
                     attn_norm_g[l].reshape(1, MLA_HEADS * V_DIM),
                     bf(w_out[l, :SSD_INNER]), bf(w_out[l, SSD_INNER:]))

        x = _ffn(x, mod, norm_g[l, 2].reshape(1, d), bf(ffn_w_gate[l, 1]), bf(ffn_w_up[l, 1]),
                 bf(ffn_w_down[l, 1]), fg, row0=6, final=(l == depth - 1))
    return x
```

```python
import functools
import math

import jax
import jax.numpy as jnp
import numpy as np
from jax import lax
from jax.experimental import pallas as pl
from jax.experimental.pallas import tpu as pltpu

F32 = jnp.float32
BF16 = jnp.bfloat16

D_MODEL = 1024
D_FF = 2816
N_MOD = 9
SSD_INNER = 1024
SSD_HEADDIM = 64
SSD_HEADS = 16
SSD_GROUPS = 2
SSD_STATE = 128
SSD_CONV = 5
SSD_CHUNK = 128
CONV_CH = SSD_INNER + 2 * SSD_GROUPS * SSD_STATE
MLA_HEADS = 8
QK_NOPE = 128
QK_ROPE = 64
V_DIM = 128
V_EXT = V_DIM + 16
Q_LORA = 512
KV_LORA = 256
ROPE_THETA = 10000.0
EPS = 1e-6

LANES = 128
SUBLANES = 8
HEAD_PAD = 256
VMEM_LIMIT = 56 * 1024 * 1024

C_Z = 0
C_XBC = C_Z + SSD_INNER
C_DT = C_XBC + CONV_CH
C_Q = C_DT + LANES
C_KV = C_Q + Q_LORA
C_KRA = C_KV + KV_LORA
C_KRB = C_KRA + LANES
IN_COLS_EXT = C_KRB + LANES

TM_FFN = 512
TM_MIX = 256
TM_OUT = 512
TQ = 512
TK_ATTN = 4096
TK_SUB = 512
ROPE_TS = 1024
CONV_HALO = SUBLANES
Q_SCALE = (QK_NOPE + QK_ROPE) ** -0.5 * math.log2(math.e)


def _dot(a, b):
    return jnp.dot(a, b, preferred_element_type=F32)


def _dot_nt(a, b):
    return lax.dot_general(a, b, (((1,), (1,)), ((), ())), preferred_element_type=F32)


def _rms(x, g):
    return x * lax.rsqrt(jnp.mean(x * x, axis=-1, keepdims=True) + EPS) * g


def _modulate(x, g, shift, scale):
    return _rms(x, g) * (1.0 + scale) + shift


def _sigmoid(x):
    return 1.0 / (1.0 + jnp.exp(-x))


def _const_spec(shape):
    nd = len(shape)
    return pl.BlockSpec(shape, lambda *_: (0,) * nd, pipeline_mode=pl.Buffered(1))


def _ada_kernel(c_ref, w_ref, b_ref, o_ref):
    c = c_ref[...]
    c_act = (c * _sigmoid(c)).astype(BF16)
    o_ref[0] = _dot(c_act, w_ref[0].astype(BF16)) + b_ref[0]


def _ada(c_pad, ada_w, ada_b):
    depth, d, n = ada_w.shape
    tn = 1536
    return pl.pallas_call(
        _ada_kernel,
        out_shape=jax.ShapeDtypeStruct((depth, SUBLANES, n), F32),
        grid=(depth, n // tn),
        in_specs=[
            pl.BlockSpec((SUBLANES, d), lambda l, j: (0, 0)),
            pl.BlockSpec((1, d, tn), lambda l, j: (l, 0, j)),
            pl.BlockSpec((1, 1, tn), lambda l, j: (l, 0, j)),
        ],
        out_specs=pl.BlockSpec((1, SUBLANES, tn), lambda l, j: (l, 0, j)),
        compiler_params=pltpu.CompilerParams(
            dimension_semantics=("arbitrary", "arbitrary"), vmem_limit_bytes=VMEM_LIMIT),
        name="ada",
    )(c_pad, ada_w, ada_b.reshape(depth, 1, n))


def _rope_kernel(pos_ref, inv_ref, cos_t_ref, sin_t_ref, cos_n_ref, sin_n_ref):
    ang = inv_ref[...] * pos_ref[0].astype(F32)
    c, s = jnp.cos(ang), jnp.sin(ang)
    zero = jnp.zeros((LANES - QK_ROPE, ang.shape[1]), F32)
    c_t = jnp.concatenate([c, c, zero], axis=0)
    s_t = jnp.concatenate([s, s, zero], axis=0)
    cos_t_ref[0] = c_t
    sin_t_ref[0] = s_t
    cos_n_ref[0] = c_t.T
    sin_n_ref[0] = s_t.T


def _rope_tables(positions):
    b, s = positions.shape
    inv_freq = 1.0 / (ROPE_THETA ** (jnp.arange(0, QK_ROPE, 2, dtype=F32) / QK_ROPE))
    t_shape = jax.ShapeDtypeStruct((b, LANES, s), F32)
    n_shape = jax.ShapeDtypeStruct((b, s, LANES), F32)
    t_spec = pl.BlockSpec((1, LANES, ROPE_TS), lambda i, j: (i, 0, j))
    n_spec = pl.BlockSpec((1, ROPE_TS, LANES), lambda i, j: (i, j, 0))
    return pl.pallas_call(
        _rope_kernel,
        out_shape=(t_shape, t_shape, n_shape, n_shape),
        grid=(b, s // ROPE_TS),
        in_specs=[
            pl.BlockSpec((1, 1, ROPE_TS), lambda i, j: (i, 0, j)),
            pl.BlockSpec((QK_ROPE // 2, 1), lambda i, j: (0, 0)),
        ],
        out_specs=(t_spec, t_spec, n_spec, n_spec),
        compiler_params=pltpu.CompilerParams(dimension_semantics=("arbitrary", "arbitrary")),
        name="rope_tables",
    )(positions.reshape(b, 1, s), inv_freq.reshape(QK_ROPE // 2, 1))


def _ffn_kernel(x_ref, mod_ref, g_ref, wg_ref, wu_ref, wd_ref, fg_ref, o_ref, *, row0, final):
    x = x_ref[0]
    shift = mod_ref[0, row0:row0 + 1, :]
    scale = mod_ref[0, row0 + 1:row0 + 2, :]
    gate = mod_ref[0, row0 + 2:row0 + 3, :]
    h = _modulate(x, g_ref[...], shift, scale).astype(BF16)
    a = _dot(h, wg_ref[...])
    u = _dot(h, wu_ref[...])
    act = (a * _sigmoid(a) * u).astype(BF16)
    out = x + 0.5 * gate * _dot(act, wd_ref[...])
    if final:
        out = _rms(out, fg_ref[...])
    o_ref[0] = out


def _ffn(x, mod, g, wg, wu, wd, fg, *, row0, final):
    b, s, d = x.shape
    tile = pl.BlockSpec((1, TM_FFN, d), lambda i, j: (i, j, 0))
    return pl.pallas_call(
        functools.partial(_ffn_kernel, row0=row0, final=final),
        out_shape=jax.ShapeDtypeStruct(x.shape, F32),
        grid=(b, s // TM_FFN),
        in_specs=[
            tile,
            pl.BlockSpec((1, N_MOD, d), lambda i, j: (i, 0, 0)),
            _const_spec((1, d)),
            _const_spec(wg.shape),
            _const_spec(wu.shape),
            _const_spec(wd.shape),
            _const_spec((1, d)),
        ],
        out_specs=tile,
        compiler_params=pltpu.CompilerParams(
            dimension_semantics=("arbitrary", "arbitrary"), vmem_limit_bytes=VMEM_LIMIT),
        name="ffn",
    )(x, mod, g, wg, wu, wd, fg)


def _mix_in_kernel(x_ref, xp_ref, xn_ref, mod_ref, g_ref, win_ref, cw_ref, cb_ref, dtb_ref,
                   qg_ref, wq_ref, kvg_ref, wkn_ref, wv_ref,
                   cos_t_ref, sin_t_ref, cos_n_ref, sin_n_ref,
                   z_ref, xbc_ref, dt_ref, q_ref, k_ref, v_ref, u_scr):
    j = pl.program_id(1)
    nj = pl.num_programs(1)
    tm = x_ref.shape[1]
    g = g_ref[...]
    shift = mod_ref[0, 3:4, :]
    scale = mod_ref[0, 4:5, :]

    h = _modulate(x_ref[0], g, shift, scale).astype(BF16)
    lat = _dot(h, win_ref[:, C_DT:])
    lat_off = lambda c: c - C_DT

    x_halo = jnp.concatenate([xp_ref[0], xn_ref[0]], axis=0)
    h_halo = _modulate(x_halo, g, shift, scale).astype(BF16)
    u_halo = _dot(h_halo, win_ref[:, C_XBC:C_DT])
    u_scr[0:CONV_HALO, :] = jnp.where(j > 0, u_halo[:CONV_HALO], 0.0)
    u_scr[CONV_HALO:CONV_HALO + tm, :] = _dot(h, win_ref[:, C_XBC:C_DT])
    u_scr[CONV_HALO + tm:, :] = jnp.where(j < nj - 1, u_halo[CONV_HALO:], 0.0)

    def conv_block(c):
        cols = slice(c * LANES, (c + 1) * LANES)
        acc = jnp.broadcast_to(cb_ref[:, cols], (tm, LANES))
        for k in range(SSD_CONV):
            off = CONV_HALO - SSD_CONV // 2 + k
            acc = acc + cw_ref[k:k + 1, cols] * u_scr[off:off + tm, cols]
        xbc_ref[0, :, cols] = acc * _sigmoid(acc)

    n_conv = CONV_CH // LANES
    dt_ref[0] = jax.nn.softplus(lat[:, lat_off(C_DT):lat_off(C_Q)] + dtb_ref[...])
    hq = _rms(lat[:, lat_off(C_Q):lat_off(C_KV)], qg_ref[...]).astype(BF16)
    ckv = _rms(lat[:, lat_off(C_KV):lat_off(C_KRA)], kvg_ref[...]).astype(BF16)
    k_rope = (lat[:, lat_off(C_KRA):lat_off(C_KRB)] * cos_n_ref[0]
              + lat[:, lat_off(C_KRB):] * sin_n_ref[0]).astype(BF16)

    half = QK_ROPE // 2
    qk_dim = QK_NOPE + QK_ROPE
    cos_t = cos_t_ref[0, 0:half, :] * Q_SCALE
    sin_t = sin_t_ref[0, 0:half, :] * Q_SCALE
    for hd in range(MLA_HEADS):
        q_h = _dot_nt(wq_ref[hd * qk_dim:(hd + 1) * qk_dim, :], hq)
        v_h = _dot_nt(wv_ref[hd * V_DIM:(hd + 1) * V_DIM, :], ckv)
        if hd % 2 == 0:
            k_pair = _dot(ckv, wkn_ref[:, hd * QK_NOPE:(hd + 2) * QK_NOPE])
        conv_block(hd)
        q_ref[0, hd, 0:LANES, :] = (q_h[0:QK_NOPE] * Q_SCALE).astype(BF16)
        t1 = q_h[QK_NOPE:QK_NOPE + half]
        t2 = q_h[QK_NOPE + half:]
        q_ref[0, hd, LANES:LANES + half, :] = (t1 * cos_t - t2 * sin_t).astype(BF16)
        q_ref[0, hd, LANES + half:LANES + QK_ROPE, :] = (t2 * cos_t + t1 * sin_t).astype(BF16)
        q_ref[0, hd, LANES + QK_ROPE:, :] = jnp.zeros((HEAD_PAD - LANES - QK_ROPE, tm), BF16)
        k_ref[0, hd, :, 0:LANES] = k_pair[:, (hd % 2) * QK_NOPE:(hd % 2 + 1) * QK_NOPE].astype(BF16)
        k_ref[0, hd, :, LANES:] = k_rope
        v_ref[0, hd, 0:V_DIM, :] = v_h.astype(BF16)
        v_ref[0, hd, V_DIM:, :] = jnp.ones((V_EXT - V_DIM, tm), BF16)

    z_cols = 2 * LANES
    for c in range(SSD_INNER // z_cols):
        z_ref[0, :, c * z_cols:(c + 1) * z_cols] = _dot(
            h, win_ref[:, C_Z + c * z_cols:C_Z + (c + 1) * z_cols]).astype(BF16)
        conv_block(MLA_HEADS + c)


def _mix_in(x, mod, g, win, cw, cb, dtb, qg, wq, kvg, wkn, wv, tabs):
    b, s, d = x.shape
    tm = TM_MIX
    cos_t, sin_t, cos_n, sin_n = tabs
    nblk = s // CONV_HALO
    per = tm // CONV_HALO
    out_shape = (
        jax.ShapeDtypeStruct((b, s, SSD_INNER), BF16),
        jax.ShapeDtypeStruct((b, s, CONV_CH), F32),
        jax.ShapeDtypeStruct((b, s, LANES), F32),
        jax.ShapeDtypeStruct((b, MLA_HEADS, HEAD_PAD, s), BF16),
        jax.ShapeDtypeStruct((b, MLA_HEADS, s, HEAD_PAD), BF16),
        jax.ShapeDtypeStruct((b, MLA_HEADS, V_EXT, s), BF16),
    )
    tok = lambda w: pl.BlockSpec((1, tm, w), lambda i, j: (i, j, 0))
    in_specs = [
        tok(d),
        pl.BlockSpec((1, CONV_HALO, d), lambda i, j: (i, jnp.maximum(j * per - 1, 0), 0)),
        pl.BlockSpec((1, CONV_HALO, d), lambda i, j: (i, jnp.minimum((j + 1) * per, nblk - 1), 0)),
        pl.BlockSpec((1, N_MOD, d), lambda i, j: (i, 0, 0)),
        _const_spec((1, d)),
        _const_spec(win.shape),
        _const_spec(cw.shape),
        _const_spec(cb.shape),
        _const_spec(dtb.shape),
        _const_spec(qg.shape),
        _const_spec(wq.shape),
        _const_spec(kvg.shape),
        _const_spec(wkn.shape),
        _const_spec(wv.shape),
        pl.BlockSpec((1, LANES, tm), lambda i, j: (i, 0, j)),
        pl.BlockSpec((1, LANES, tm), lambda i, j: (i, 0, j)),
        tok(LANES),
        tok(LANES),
    ]
    out_specs = (
        tok(SSD_INNER),
        tok(CONV_CH),
        tok(LANES),
        pl.BlockSpec((1, MLA_HEADS, HEAD_PAD, tm), lambda i, j: (i, 0, 0, j)),
        pl.BlockSpec((1, MLA_HEADS, tm, HEAD_PAD), lambda i, j: (i, 0, j, 0)),
        pl.BlockSpec((1, MLA_HEADS, V_EXT, tm), lambda i, j: (i, 0, 0, j)),
    )
    return pl.pallas_call(
        _mix_in_kernel,
        out_shape=out_shape,
        grid=(b, s // tm),
        in_specs=in_specs,
        out_specs=out_specs,
        scratch_shapes=[pltpu.VMEM((tm + 2 * CONV_HALO, CONV_CH), F32)],
        compiler_params=pltpu.CompilerParams(
            dimension_semantics=("arbitrary", "arbitrary"), vmem_limit_bytes=VMEM_LIMIT),
        name="mix_in",
    )(x, x, x, mod, g, win, cw, cb, dtb, qg, wq, kvg, wkn, wv, cos_t, sin_t, cos_n, sin_n)


def _split3(x):
    hi = x.astype(BF16)
    r1 = x - hi.astype(F32)
    mid = r1.astype(BF16)
    lo = (r1 - mid.astype(F32)).astype(BF16)
    return hi, mid, lo


def _ssd_direction(xbc, dt, a_row, state_ref, y_ref, d_row, *, backward):
    L = SSD_CHUNK
    pair_w = 2 * SSD_HEADDIM
    pairs_per_group = SSD_HEADS // SSD_GROUPS // 2
    off = SSD_HEADS if backward else 0
    edge = 0 if backward else L - 1
    row_i = lax.broadcasted_iota(jnp.int32, (L, L), 0)
    col_i = lax.broadcasted_iota(jnp.int32, (L, L), 1)
    keep = (row_i <= col_i) if backward else (row_i >= col_i)
    left = col_i < SSD_HEADDIM
    tri = jnp.where(keep, 1.0, 0.0).astype(BF16)
    a_dt = dt * a_row
    hi, mid, lo = _split3(a_dt)
    a_cs = _dot(tri, hi) + _dot(tri, mid) + _dot(tri, lo)
    a_cs_t = a_cs.T
    dt_t = dt.T
    c_off = SSD_INNER + SSD_GROUPS * SSD_STATE
    for grp in range(SSD_GROUPS):
        b_g = xbc[:, SSD_INNER + grp * SSD_STATE:SSD_INNER + (grp + 1) * SSD_STATE]
        c_bf = xbc[:, c_off + grp * SSD_STATE:c_off + (grp + 1) * SSD_STATE].astype(BF16)
        cb = _dot_nt(c_bf, b_g.astype(BF16))
        b_t = b_g.T
        states = [state_ref[grp * pairs_per_group + j] for j in range(pairs_per_group)]
        y_off = _dot(c_bf, jnp.concatenate(states, axis=1).astype(BF16))
        for j in range(pairs_per_group):
            pr = grp * pairs_per_group + j
            xs_pair = xbc[:, pr * pair_w:(pr + 1) * pair_w]
            rhs = jnp.concatenate([jnp.where(left, xs_pair, 0.0).astype(BF16),
                                   jnp.where(left, 0.0, xs_pair).astype(BF16)], axis=0)
            m_parts, w_parts, col_parts, tot_parts = [], [], [], []
            for hd in (2 * pr, 2 * pr + 1):
                col = jnp.broadcast_to(a_cs[:, off + hd:off + hd + 1], (L, L))
                row = a_cs_t[off + hd:off + hd + 1, :]
                dt_row = dt_t[off + hd:off + hd + 1, :]
                decay = jnp.exp(jnp.where(keep, col - row, -jnp.inf))
                m_parts.append((cb * decay * dt_row).astype(BF16))
                a_tot = row[:, edge:edge + 1]
                w_parts.append((b_t * (dt_row * jnp.exp(a_tot - row))).astype(BF16))
                col_parts.append(col)
                tot_parts.append(a_tot)
            y = _dot(jnp.concatenate(m_parts, axis=1), rhs)
            y = y + jnp.exp(jnp.where(left, col_parts[0], col_parts[1])) \
                * y_off[:, j * pair_w:(j + 1) * pair_w]
            if d_row is not None:
                y = y + d_row[:, pr * pair_w:(pr + 1) * pair_w] * xs_pair
            y_ref[0, :, pr * pair_w:(pr + 1) * pair_w] = y.astype(BF16)
            upd = _dot(jnp.concatenate(w_parts, axis=1), rhs)
            keep_frac = jnp.exp(jnp.where(left[0:1], tot_parts[0], tot_parts[1]))
            state_ref[pr] = keep_frac * states[j] + upd


def _ssd_kernel(xf_ref, xb_ref, dtf_ref, dtb_ref, alog_ref, d_ref, yf_ref, yb_ref, sf_ref, sb_ref):
    @pl.when(pl.program_id(1) == 0)
    def _():
        sf_ref[...] = jnp.zeros_like(sf_ref)
        sb_ref[...] = jnp.zeros_like(sb_ref)

    a_row = -jnp.exp(alog_ref[...])
    _ssd_direction(xf_ref[0], dtf_ref[0], a_row, sf_ref, yf_ref, d_ref[...], backward=False)
    _ssd_direction(xb_ref[0], dtb_ref[0], a_row, sb_ref, yb_ref, None, backward=True)


def _ssd(xbc, dt, alog_pad, d_exp):
    b, s, _ = xbc.shape
    nc = s // SSD_CHUNK
    fwd = lambda w: pl.BlockSpec((1, SSD_CHUNK, w), lambda i, c: (i, c, 0))
    bwd = lambda w: pl.BlockSpec((1, SSD_CHUNK, w), lambda i, c: (i, nc - 1 - c, 0))
    y_shape = jax.ShapeDtypeStruct((b, s, SSD_INNER), BF16)
    state = pltpu.VMEM((SSD_HEADS // 2, SSD_STATE, 2 * SSD_HEADDIM), F32)
    return pl.pallas_call(
        _ssd_kernel,
        out_shape=(y_shape, y_shape),
        grid=(b, nc),
        in_specs=[fwd(CONV_CH), bwd(CONV_CH), fwd(LANES), bwd(LANES),
                  _const_spec((1, LANES)), _const_spec((1, SSD_INNER))],
        out_specs=(fwd(SSD_INNER), bwd(SSD_INNER)),
        scratch_shapes=[state, state],
        compiler_params=pltpu.CompilerParams(
            dimension_semantics=("arbitrary", "arbitrary"), vmem_limit_bytes=VMEM_LIMIT),
        name="ssd",
    )(xbc, xbc, dt, dt, alog_pad, d_exp)


def _fold_rows(t, op):
    r, c = t.shape
    return op(t.reshape(r // SUBLANES, SUBLANES, c), axis=0)


def _attn_kernel(q_ref, k_ref, v_ref, o_ref, sa_scr, sb_scr, ma_scr, mb_scr):
    t = pl.program_id(0)
    n_items = pl.num_programs(0) - 1
    q_t = q_ref[0]
    tq = q_t.shape[1]
    n_steps = k_ref.shape[1] // TK_ATTN
    n_sub = TK_ATTN // TK_SUB

    def scores(s_new, j, m_run, tk):
        start = pl.multiple_of(j * tk, tk)
        s_t = _dot(k_ref[0, pl.ds(start, tk), :], q_t)
        s_new[pl.ds(start, tk), :] = s_t
        return jnp.maximum(m_run, _fold_rows(s_t, jnp.max))

    def values(s_old, j, m, acc, tk):
        start = pl.multiple_of(j * tk, tk)
        p_t = jnp.exp2(s_old[pl.ds(start, tk), :] - m)
        return acc + _dot(v_ref[0, :, pl.ds(start, tk)], p_t.astype(BF16))

    m_init = jnp.full((SUBLANES, tq), -jnp.inf, F32)
    acc_init = jnp.zeros((V_EXT, tq), F32)

    def finish_values(acc):
        o_ref[0] = (acc[:V_DIM] / acc[V_DIM:V_DIM + 1]).T.astype(BF16)

    def stage(s_new, m_new, s_old, m_old):
        @pl.when(t == 0)
        def _():
            m_new[...] = lax.fori_loop(
                0, n_steps, lambda i, m_run: scores(s_new, i, m_run, TK_ATTN), m_init)

        @pl.when(jnp.logical_and(t > 0, t < n_items))
        def _():
            m = jnp.max(m_old[...], axis=0, keepdims=True)

            def both(i, carry):
                m_run, acc = carry
                for u in range(n_sub):
                    m_run = scores(s_new, i * n_sub + u, m_run, TK_SUB)
                    acc = values(s_old, i * n_sub + u, m, acc, TK_SUB)
                return m_run, acc

            m_run, acc = lax.fori_loop(0, n_steps, both, (m_init, acc_init))
            m_new[...] = m_run
            finish_values(acc)

        @pl.when(t == n_items)
        def _():
            m = jnp.max(m_old[...], axis=0, keepdims=True)
            finish_values(lax.fori_loop(
                0, n_steps, lambda i, acc: values(s_old, i, m, acc, TK_ATTN), acc_init))

    even = lax.rem(t, 2) == 0
    pl.when(even)(lambda: stage(sa_scr, ma_scr, sb_scr, mb_scr))
    pl.when(jnp.logical_not(even))(lambda: stage(sb_scr, mb_scr, sa_scr, ma_scr))


def _attention(q_t, k, v_t):
    b, nh, _, s = q_t.shape
    n_q = s // TQ
    n_items = b * nh * n_q
    cur = lambda t: jnp.minimum(t, n_items - 1)
    prev = lambda t: jnp.maximum(t - 1, 0)
    return pl.pallas_call(
        _attn_kernel,
        out_shape=jax.ShapeDtypeStruct((b, s, nh * V_DIM), BF16),
        grid=(n_items + 1,),
        in_specs=[
            pl.BlockSpec((1, HEAD_PAD, TQ), lambda t: (cur(t) // n_q, 0, cur(t) % n_q)),
            pl.BlockSpec((1, s, HEAD_PAD), lambda t: (cur(t) // n_q, 0, 0)),
            pl.BlockSpec((1, V_EXT, s), lambda t: (prev(t) // n_q, 0, 0)),
        ],
        out_specs=pl.BlockSpec(
            (1, TQ, V_DIM),
            lambda t: (prev(t) // (nh * n_q), prev(t) % n_q, (prev(t) // n_q) % nh)),
        scratch_shapes=[pltpu.VMEM((s, TQ), F32), pltpu.VMEM((s, TQ), F32),
                        pltpu.VMEM((SUBLANES, TQ), F32), pltpu.VMEM((SUBLANES, TQ), F32)],
        compiler_params=pltpu.CompilerParams(
            dimension_semantics=("arbitrary",), vmem_limit_bytes=VMEM_LIMIT),
        name="attention",
    )(q_t.reshape(b * nh, HEAD_PAD, s), k.reshape(b * nh, s, HEAD_PAD),
      v_t.reshape(b * nh, V_EXT, s))


def _mix_out_kernel(x_ref, mod_ref, yf_ref, yb_ref, z_ref, at_ref, sg_ref, ag_ref,
                    wo_s_ref, wo_a_ref, o_ref):
    gate = mod_ref[0, 5:6, :]
    z = z_ref[0].astype(F32)
    y = (yf_ref[0].astype(F32) + yb_ref[0].astype(F32)) * (z * _sigmoid(z))
    gw = SSD_INNER // SSD_GROUPS
    parts = [_rms(y[:, i * gw:(i + 1) * gw], sg_ref[:, i * gw:(i + 1) * gw]).astype(BF16)
             for i in range(SSD_GROUPS)]
    ssd_out = jnp.concatenate(parts, axis=1)
    attn = _rms(at_ref[0].astype(F32), ag_ref[...]).astype(BF16)
    mixed = _dot(ssd_out, wo_s_ref[...]) + _dot(attn, wo_a_ref[...])
    o_ref[0] = x_ref[0] + gate * mixed


def _mix_out(x, mod, y_f, y_b, z, attn, sg, ag, wo_s, wo_a):
    b, s, d = x.shape
    tile = pl.BlockSpec((1, TM_OUT, d), lambda i, j: (i, j, 0))
    return pl.pallas_call(
        _mix_out_kernel,
        out_shape=jax.ShapeDtypeStruct(x.shape, F32),
        grid=(b, s // TM_OUT),
        in_specs=[tile, pl.BlockSpec((1, N_MOD, d), lambda i, j: (i, 0, 0)),
                  tile, tile, tile, tile,
                  _const_spec((1, d)), _const_spec((1, d)),
                  _const_spec(wo_s.shape), _const_spec(wo_a.shape)],
        out_specs=tile,
        compiler_params=pltpu.CompilerParams(
            dimension_semantics=("arbitrary", "arbitrary"), vmem_limit_bytes=VMEM_LIMIT),
        name="mix_out",
    )(x, mod, y_f, y_b, z, attn, sg, ag, wo_s, wo_a)


def _pad_cols(w, width):
    return jnp.pad(w, ((0, 0), (0, width - w.shape[1])))


def _prep_w_in(w_in):
    cuts = np.cumsum([SSD_INNER, CONV_CH, 2 * SSD_HEADS, Q_LORA, KV_LORA])
    w_z, w_xbc, w_dt, w_q, w_kv, w_kr = jnp.split(w_in, [int(c) for c in cuts], axis=1)
    t1, t2 = w_kr[:, :QK_ROPE // 2], w_kr[:, QK_ROPE // 2:]
    kra = _pad_cols(jnp.concatenate([t1, t2], axis=1), LANES)
    krb = _pad_cols(jnp.concatenate([-t2, t1], axis=1), LANES)
    return jnp.concatenate([w_z, w_xbc, _pad_cols(w_dt, LANES), w_q, w_kv, kra, krb],
                           axis=1).astype(BF16)


def _prep_w_q(w_q_b):
    return w_q_b.T.astype(BF16)


def _prep_w_kv(w_kv_b):
    w = w_kv_b.reshape(KV_LORA, MLA_HEADS, QK_NOPE + V_DIM)
    w_kn = w[:, :, :QK_NOPE].reshape(KV_LORA, MLA_HEADS * QK_NOPE).astype(BF16)
    w_v_t = w[:, :, QK_NOPE:].reshape(KV_LORA, MLA_HEADS * V_DIM).T.astype(BF16)
    return w_kn, w_v_t


def kernel(x, c, positions, ada_w, ada_b, norm_g, ffn_w_gate, ffn_w_up, ffn_w_down, w_in, conv_w,
           conv_b, dt_bias, a_log, d_skip, ssd_norm_g, q_norm_g, w_q_b, kv_norm_g, w_kv_b,
           attn_norm_g, w_out, final_norm_g):
    b, s, d = x.shape
    depth = ada_w.shape[0]
    assert s % TM_FFN == 0 and s % TM_MIX == 0 and s % TM_OUT == 0 and s % SSD_CHUNK == 0
    assert s % TQ == 0 and s % TK_ATTN == 0 and s % ROPE_TS == 0

    c_pad = jnp.pad(c, ((0, SUBLANES - b), (0, 0)))
    mod_all = _ada(c_pad, ada_w, ada_b)
    tabs = _rope_tables(positions)
    fg = final_norm_g.reshape(1, d)

    for l in range(depth):
        mod = mod_all[l, :b].reshape(b, N_MOD, d)
        bf = lambda w: w.astype(BF16)
        x = _ffn(x, mod, norm_g[l, 0].reshape(1, d), bf(ffn_w_gate[l, 0]), bf(ffn_w_up[l, 0]),
                 bf(ffn_w_down[l, 0]), fg, row0=0, final=False)

        w_kn, w_v_t = _prep_w_kv(w_kv_b[l])
        dtb = _pad_cols(dt_bias[l].reshape(1, 2 * SSD_HEADS), LANES)
        z, xbc, dt, q_t, k, v_t = _mix_in(
            x, mod, norm_g[l, 1].reshape(1, d), _prep_w_in(w_in[l]), conv_w[l],
            conv_b[l].reshape(1, CONV_CH), dtb, q_norm_g[l].reshape(1, Q_LORA), _prep_w_q(w_q_b[l]),
            kv_norm_g[l].reshape(1, KV_LORA), w_kn, w_v_t, tabs)

        alog_pad = _pad_cols(a_log[l].reshape(1, 2 * SSD_HEADS), LANES)
        d_exp = jnp.repeat(d_skip[l], SSD_HEADDIM).reshape(1, SSD_INNER)
        y_f, y_b = _ssd(xbc, dt, alog_pad, d_exp)
        attn = _attention(q_t, k, v_t)

        x = _mix_out(x, mod, y_f, y_b, z, attn, ssd_norm_g[l].reshape(1, SSD_INNER),
                     attn_norm_g[l].reshape(1, MLA_HEADS * V_DIM),
                     bf(w_out[l, :SSD_INNER]), bf(w_out[l, SSD_INNER:]))

        x = _ffn(x, mod, norm_g[l, 2].reshape(1, d), bf(ffn_w_gate[l, 1]), bf(ffn_w_up[l, 1]),
                 bf(ffn_w_down[l, 1]), fg, row0=6, final=(l == depth - 1))
    return x
```

```python
import functools
import math

import jax
import jax.numpy as jnp
import numpy as np
from jax import lax
from jax.experimental import pallas as pl
from jax.experimental.pallas import tpu as pltpu

F32 = jnp.float32
BF16 = jnp.bfloat16

D_MODEL = 1024
D_FF = 2816
N_MOD = 9
SSD_INNER = 1024
SSD_HEADDIM = 64
SSD_HEADS = 16
SSD_GROUPS = 2
SSD_STATE = 128
SSD_CONV = 5
SSD_CHUNK = 128
CONV_CH = SSD_INNER + 2 * SSD_GROUPS * SSD_STATE
MLA_HEADS = 8
QK_NOPE = 128
QK_ROPE = 64
V_DIM = 128
V_EXT = V_DIM + 16
Q_LORA = 512
KV_LORA = 256
ROPE_THETA = 10000.0
EPS = 1e-6

LANES = 128
SUBLANES = 8
HEAD_PAD = 256
VMEM_LIMIT = 56 * 1024 * 1024

C_Z = 0
C_XBC = C_Z + SSD_INNER
C_DT = C_XBC + CONV_CH
C_Q = C_DT + LANES
C_KV = C_Q + Q_LORA
C_KRA = C_KV + KV_LORA
C_KRB = C_KRA + LANES
IN_COLS_EXT = C_KRB + LANES

TM_FFN = 512
TM_MIX = 256
TM_OUT = 512
ROWS_OUT = 256
TQ = 512
TK_ATTN = 4096
TK_SUB = 512
ROPE_TS = 1024
CONV_HALO = SUBLANES
Q_SCALE = (QK_NOPE + QK_ROPE) ** -0.5 * math.log2(math.e)


def _dot(a, b):
    return jnp.dot(a, b, preferred_element_type=F32)


def _dot_nt(a, b):
    return lax.dot_general(a, b, (((1,), (1,)), ((), ())), preferred_element_type=F32)


def _rms(x, g):
    return x * lax.rsqrt(jnp.mean(x * x, axis=-1, keepdims=True) + EPS) * g


def _modulate(x, g, shift, scale):
    return _rms(x, g) * (1.0 + scale) + shift


def _sigmoid(x):
    return 1.0 / (1.0 + jnp.exp(-x))


def _const_spec(shape):
    nd = len(shape)
    return pl.BlockSpec(shape, lambda *_: (0,) * nd, pipeline_mode=pl.Buffered(1))


def _ada_kernel(c_ref, w_ref, b_ref, o_ref):
    c = c_ref[...]
    c_act = (c * _sigmoid(c)).astype(BF16)
    o_ref[0] = _dot(c_act, w_ref[0].astype(BF16)) + b_ref[0]


def _ada(c_pad, ada_w, ada_b):
    depth, d, n = ada_w.shape
    tn = 1536
    return pl.pallas_call(
        _ada_kernel,
        out_shape=jax.ShapeDtypeStruct((depth, SUBLANES, n), F32),
        grid=(depth, n // tn),
        in_specs=[
            pl.BlockSpec((SUBLANES, d), lambda l, j: (0, 0)),
            pl.BlockSpec((1, d, tn), lambda l, j: (l, 0, j)),
            pl.BlockSpec((1, 1, tn), lambda l, j: (l, 0, j)),
        ],
        out_specs=pl.BlockSpec((1, SUBLANES, tn), lambda l, j: (l, 0, j)),
        compiler_params=pltpu.CompilerParams(
            dimension_semantics=("arbitrary", "arbitrary"), vmem_limit_bytes=VMEM_LIMIT),
        name="ada",
    )(c_pad, ada_w, ada_b.reshape(depth, 1, n))


def _rope_kernel(pos_ref, inv_ref, cos_t_ref, sin_t_ref, cos_n_ref, sin_n_ref):
    ang = inv_ref[...] * pos_ref[0].astype(F32)
    c, s = jnp.cos(ang), jnp.sin(ang)
    zero = jnp.zeros((LANES - QK_ROPE, ang.shape[1]), F32)
    c_t = jnp.concatenate([c, c, zero], axis=0)
    s_t = jnp.concatenate([s, s, zero], axis=0)
    cos_t_ref[0] = c_t
    sin_t_ref[0] = s_t
    cos_n_ref[0] = c_t.T
    sin_n_ref[0] = s_t.T


def _rope_tables(positions):
    b, s = positions.shape
    inv_freq = 1.0 / (ROPE_THETA ** (jnp.arange(0, QK_ROPE, 2, dtype=F32) / QK_ROPE))
    t_shape = jax.ShapeDtypeStruct((b, LANES, s), F32)
    n_shape = jax.ShapeDtypeStruct((b, s, LANES), F32)
    t_spec = pl.BlockSpec((1, LANES, ROPE_TS), lambda i, j: (i, 0, j))
    n_spec = pl.BlockSpec((1, ROPE_TS, LANES), lambda i, j: (i, j, 0))
    return pl.pallas_call(
        _rope_kernel,
        out_shape=(t_shape, t_shape, n_shape, n_shape),
        grid=(b, s // ROPE_TS),
        in_specs=[
            pl.BlockSpec((1, 1, ROPE_TS), lambda i, j: (i, 0, j)),
            pl.BlockSpec((QK_ROPE // 2, 1), lambda i, j: (0, 0)),
        ],
        out_specs=(t_spec, t_spec, n_spec, n_spec),
        compiler_params=pltpu.CompilerParams(dimension_semantics=("arbitrary", "arbitrary")),
        name="rope_tables",
    )(positions.reshape(b, 1, s), inv_freq.reshape(QK_ROPE // 2, 1))


def _ffn_kernel(x_ref, mod_ref, g_ref, wg_ref, wu_ref, wd_ref, fg_ref, o_ref, *, row0, final):
    x = x_ref[0]
    shift = mod_ref[0, row0:row0 + 1, :]
    scale = mod_ref[0, row0 + 1:row0 + 2, :]
    gate = mod_ref[0, row0 + 2:row0 + 3, :]
    h = _modulate(x, g_ref[...], shift, scale).astype(BF16)
    a = _dot(h, wg_ref[...])
    u = _dot(h, wu_ref[...])
    act = (a * _sigmoid(a) * u).astype(BF16)
    out = x + 0.5 * gate * _dot(act, wd_ref[...])
    if final:
        out = _rms(out, fg_ref[...])
    o_ref[0] = out


def _ffn(x, mod, g, wg, wu, wd, fg, *, row0, final):
    b, s, d = x.shape
    tile = pl.BlockSpec((1, TM_FFN, d), lambda i, j: (i, j, 0))
    return pl.pallas_call(
        functools.partial(_ffn_kernel, row0=row0, final=final),
        out_shape=jax.ShapeDtypeStruct(x.shape, F32),
        grid=(b, s // TM_FFN),
        in_specs=[
            tile,
            pl.BlockSpec((1, N_MOD, d), lambda i, j: (i, 0, 0)),
            _const_spec((1, d)),
            _const_spec(wg.shape),
            _const_spec(wu.shape),
            _const_spec(wd.shape),
            _const_spec((1, d)),
        ],
        out_specs=tile,
        compiler_params=pltpu.CompilerParams(
            dimension_semantics=("arbitrary", "arbitrary"), vmem_limit_bytes=VMEM_LIMIT),
        name="ffn",
    )(x, mod, g, wg, wu, wd, fg)


def _mix_in_kernel(x_ref, xp_ref, xn_ref, mod_ref, g_ref, win_ref, cw_ref, cb_ref, dtb_ref,
                   qg_ref, wq_ref, kvg_ref, wkn_ref, wv_ref,
                   cos_t_ref, sin_t_ref, cos_n_ref, sin_n_ref,
                   z_ref, xbc_ref, dt_ref, q_ref, k_ref, v_ref, u_scr):
    j = pl.program_id(1)
    nj = pl.num_programs(1)
    tm = x_ref.shape[1]
    g = g_ref[...]
    shift = mod_ref[0, 3:4, :]
    scale = mod_ref[0, 4:5, :]

    h = _modulate(x_ref[0], g, shift, scale).astype(BF16)
    lat = _dot(h, win_ref[:, C_DT:])
    lat_off = lambda c: c - C_DT

    x_halo = jnp.concatenate([xp_ref[0], xn_ref[0]], axis=0)
    h_halo = _modulate(x_halo, g, shift, scale).astype(BF16)
    u_halo = _dot(h_halo, win_ref[:, C_XBC:C_DT])
    u_scr[0:CONV_HALO, :] = jnp.where(j > 0, u_halo[:CONV_HALO], 0.0)
    u_scr[CONV_HALO:CONV_HALO + tm, :] = _dot(h, win_ref[:, C_XBC:C_DT])
    u_scr[CONV_HALO + tm:, :] = jnp.where(j < nj - 1, u_halo[CONV_HALO:], 0.0)

    def conv_block(c):
        cols = slice(c * LANES, (c + 1) * LANES)
        acc = jnp.broadcast_to(cb_ref[:, cols], (tm, LANES))
        for k in range(SSD_CONV):
            off = CONV_HALO - SSD_CONV // 2 + k
            acc = acc + cw_ref[k:k + 1, cols] * u_scr[off:off + tm, cols]
        xbc_ref[0, :, cols] = acc * _sigmoid(acc)

    n_conv = CONV_CH // LANES
    dt_ref[0] = jax.nn.softplus(lat[:, lat_off(C_DT):lat_off(C_Q)] + dtb_ref[...])
    hq = _rms(lat[:, lat_off(C_Q):lat_off(C_KV)], qg_ref[...]).astype(BF16)
    ckv = _rms(lat[:, lat_off(C_KV):lat_off(C_KRA)], kvg_ref[...]).astype(BF16)
    k_rope = (lat[:, lat_off(C_KRA):lat_off(C_KRB)] * cos_n_ref[0]
              + lat[:, lat_off(C_KRB):] * sin_n_ref[0]).astype(BF16)

    half = QK_ROPE // 2
    qk_dim = QK_NOPE + QK_ROPE
    cos_t = cos_t_ref[0, 0:half, :] * Q_SCALE
    sin_t = sin_t_ref[0, 0:half, :] * Q_SCALE
    for hd in range(MLA_HEADS):
        q_h = _dot_nt(wq_ref[hd * qk_dim:(hd + 1) * qk_dim, :], hq)
        v_h = _dot_nt(wv_ref[hd * V_DIM:(hd + 1) * V_DIM, :], ckv)
        if hd % 2 == 0:
            k_pair = _dot(ckv, wkn_ref[:, hd * QK_NOPE:(hd + 2) * QK_NOPE])
        conv_block(hd)
        q_ref[0, hd, 0:LANES, :] = (q_h[0:QK_NOPE] * Q_SCALE).astype(BF16)
        t1 = q_h[QK_NOPE:QK_NOPE + half]
        t2 = q_h[QK_NOPE + half:]
        q_ref[0, hd, LANES:LANES + half, :] = (t1 * cos_t - t2 * sin_t).astype(BF16)
        q_ref[0, hd, LANES + half:LANES + QK_ROPE, :] = (t2 * cos_t + t1 * sin_t).astype(BF16)
        q_ref[0, hd, LANES + QK_ROPE:, :] = jnp.zeros((HEAD_PAD - LANES - QK_ROPE, tm), BF16)
        k_ref[0, hd, :, 0:LANES] = k_pair[:, (hd % 2) * QK_NOPE:(hd % 2 + 1) * QK_NOPE].astype(BF16)
        k_ref[0, hd, :, LANES:] = k_rope
        v_ref[0, hd, 0:V_DIM, :] = v_h.astype(BF16)
        v_ref[0, hd, V_DIM:, :] = jnp.ones((V_EXT - V_DIM, tm), BF16)

    z_cols = 2 * LANES
    for c in range(SSD_INNER // z_cols):
        z_ref[0, :, c * z_cols:(c + 1) * z_cols] = _dot(
            h, win_ref[:, C_Z + c * z_cols:C_Z + (c + 1) * z_cols]).astype(BF16)
        conv_block(MLA_HEADS + c)


def _mix_in(x, mod, g, win, cw, cb, dtb, qg, wq, kvg, wkn, wv, tabs):
    b, s, d = x.shape
    tm = TM_MIX
    cos_t, sin_t, cos_n, sin_n = tabs
    nblk = s // CONV_HALO
    per = tm // CONV_HALO
    out_shape = (
        jax.ShapeDtypeStruct((b, s, SSD_INNER), BF16),
        jax.ShapeDtypeStruct((b, s, CONV_CH), F32),
        jax.ShapeDtypeStruct((b, s, LANES), F32),
        jax.ShapeDtypeStruct((b, MLA_HEADS, HEAD_PAD, s), BF16),
        jax.ShapeDtypeStruct((b, MLA_HEADS, s, HEAD_PAD), BF16),
        jax.ShapeDtypeStruct((b, MLA_HEADS, V_EXT, s), BF16),
    )
    tok = lambda w: pl.BlockSpec((1, tm, w), lambda i, j: (i, j, 0))
    in_specs = [
        tok(d),
        pl.BlockSpec((1, CONV_HALO, d), lambda i, j: (i, jnp.maximum(j * per - 1, 0), 0)),
        pl.BlockSpec((1, CONV_HALO, d), lambda i, j: (i, jnp.minimum((j + 1) * per, nblk - 1), 0)),
        pl.BlockSpec((1, N_MOD, d), lambda i, j: (i, 0, 0)),
        _const_spec((1, d)),
        _const_spec(win.shape),
        _const_spec(cw.shape),
        _const_spec(cb.shape),
        _const_spec(dtb.shape),
        _const_spec(qg.shape),
        _const_spec(wq.shape),
        _const_spec(kvg.shape),
        _const_spec(wkn.shape),
        _const_spec(wv.shape),
        pl.BlockSpec((1, LANES, tm), lambda i, j: (i, 0, j)),
        pl.BlockSpec((1, LANES, tm), lambda i, j: (i, 0, j)),
        tok(LANES),
        tok(LANES),
    ]
    out_specs = (
        tok(SSD_INNER),
        tok(CONV_CH),
        tok(LANES),
        pl.BlockSpec((1, MLA_HEADS, HEAD_PAD, tm), lambda i, j: (i, 0, 0, j)),
        pl.BlockSpec((1, MLA_HEADS, tm, HEAD_PAD), lambda i, j: (i, 0, j, 0)),
        pl.BlockSpec((1, MLA_HEADS, V_EXT, tm), lambda i, j: (i, 0, 0, j)),
    )
    return pl.pallas_call(
        _mix_in_kernel,
        out_shape=out_shape,
        grid=(b, s // tm),
        in_specs=in_specs,
        out_specs=out_specs,
        scratch_shapes=[pltpu.VMEM((tm + 2 * CONV_HALO, CONV_CH), F32)],
        compiler_params=pltpu.CompilerParams(
            dimension_semantics=("arbitrary", "arbitrary"), vmem_limit_bytes=VMEM_LIMIT),
        name="mix_in",
    )(x, x, x, mod, g, win, cw, cb, dtb, qg, wq, kvg, wkn, wv, cos_t, sin_t, cos_n, sin_n)


def _split3(x):
    hi = x.astype(BF16)
    r1 = x - hi.astype(F32)
    mid = r1.astype(BF16)
    lo = (r1 - mid.astype(F32)).astype(BF16)
    return hi, mid, lo


def _ssd_direction(xbc, dt, a_row, state_ref, y_ref, d_row, *, backward):
    L = SSD_CHUNK
    pair_w = 2 * SSD_HEADDIM
    pairs_per_group = SSD_HEADS // SSD_GROUPS // 2
    off = SSD_HEADS if backward else 0
    edge = 0 if backward else L - 1
    row_i = lax.broadcasted_iota(jnp.int32, (L, L), 0)
    col_i = lax.broadcasted_iota(jnp.int32, (L, L), 1)
    keep = (row_i <= col_i) if backward else (row_i >= col_i)
    left = col_i < SSD_HEADDIM
    tri = jnp.where(keep, 1.0, 0.0).astype(BF16)
    a_dt = dt * a_row
    hi, mid, lo = _split3(a_dt)
    a_cs = _dot(tri, hi) + _dot(tri, mid) + _dot(tri, lo)
    a_cs_t = a_cs.T
    src_t = (a_cs - jnp.log2(dt)).T
    yield
    c_off = SSD_INNER + SSD_GROUPS * SSD_STATE
    for grp in range(SSD_GROUPS):
        b_g = xbc[:, SSD_INNER + grp * SSD_STATE:SSD_INNER + (grp + 1) * SSD_STATE]
        c_bf = xbc[:, c_off + grp * SSD_STATE:c_off + (grp + 1) * SSD_STATE].astype(BF16)
        cb = _dot_nt(c_bf, b_g.astype(BF16))
        b_t = b_g.T
        states = [state_ref[grp * pairs_per_group + j] for j in range(pairs_per_group)]
        y_off = _dot(c_bf, jnp.concatenate(states, axis=1).astype(BF16))
        yield
        for j in range(pairs_per_group):
            pr = grp * pairs_per_group + j
            xs_pair = xbc[:, pr * pair_w:(pr + 1) * pair_w]
            rhs = jnp.concatenate([jnp.where(left, xs_pair, 0.0).astype(BF16),
                                   jnp.where(left, 0.0, xs_pair).astype(BF16)], axis=0)
            m_parts, w_parts, col_parts, tot_parts = [], [], [], []
            for hd in (2 * pr, 2 * pr + 1):
                col = jnp.broadcast_to(a_cs[:, off + hd:off + hd + 1], (L, L))
                src = src_t[off + hd:off + hd + 1, :]
                decay_dt = jnp.exp2(jnp.where(keep, col - src, -jnp.inf))
                m_parts.append((cb * decay_dt).astype(BF16))
                a_tot = a_cs_t[off + hd:off + hd + 1, edge:edge + 1]
                w_parts.append((b_t * jnp.exp2(a_tot - src)).astype(BF16))
                col_parts.append(col)
                tot_parts.append(a_tot)
            y = _dot(jnp.concatenate(m_parts, axis=1), rhs)
            upd = _dot(jnp.concatenate(w_parts, axis=1), rhs)
            yield
            y = y + jnp.exp2(jnp.where(left, col_parts[0], col_parts[1])) \
                * y_off[:, j * pair_w:(j + 1) * pair_w]
            if d_row is not None:
                y = y + d_row[:, pr * pair_w:(pr + 1) * pair_w] * xs_pair
            y_ref[0, :, pr * pair_w:(pr + 1) * pair_w] = y.astype(BF16)
            keep_frac = jnp.exp2(jnp.where(left[0:1], tot_parts[0], tot_parts[1]))
            state_ref[pr] = keep_frac * states[j] + upd
            yield


def _ssd_kernel(xf_ref, xb_ref, dtf_ref, dtb_ref, alog_ref, d_ref, yf_ref, yb_ref, sf_ref, sb_ref):
    @pl.when(pl.program_id(1) == 0)
    def _():
        sf_ref[...] = jnp.zeros_like(sf_ref)
        sb_ref[...] = jnp.zeros_like(sb_ref)

    a_row = -jnp.exp(alog_ref[...]) * math.log2(math.e)
    fwd = _ssd_direction(xf_ref[0], dtf_ref[0], a_row, sf_ref, yf_ref, d_ref[...], backward=False)
    bwd = _ssd_direction(xb_ref[0], dtb_ref[0], a_row, sb_ref, yb_ref, None, backward=True)
    for _ in zip(fwd, bwd):
        pass


def _ssd(xbc, dt, alog_pad, d_exp):
    b, s, _ = xbc.shape
    nc = s // SSD_CHUNK
    fwd = lambda w: pl.BlockSpec((1, SSD_CHUNK, w), lambda i, c: (i, c, 0))
    bwd = lambda w: pl.BlockSpec((1, SSD_CHUNK, w), lambda i, c: (i, nc - 1 - c, 0))
    y_shape = jax.ShapeDtypeStruct((b, s, SSD_INNER), BF16)
    state = pltpu.VMEM((SSD_HEADS // 2, SSD_STATE, 2 * SSD_HEADDIM), F32)
    return pl.pallas_call(
        _ssd_kernel,
        out_shape=(y_shape, y_shape),
        grid=(b, nc),
        in_specs=[fwd(CONV_CH), bwd(CONV_CH), fwd(LANES), bwd(LANES),
                  _const_spec((1, LANES)), _const_spec((1, SSD_INNER))],
        out_specs=(fwd(SSD_INNER), bwd(SSD_INNER)),
        scratch_shapes=[state, state],
        compiler_params=pltpu.CompilerParams(
            dimension_semantics=("arbitrary", "arbitrary"), vmem_limit_bytes=VMEM_LIMIT),
        name="ssd",
    )(xbc, xbc, dt, dt, alog_pad, d_exp)


def _fold_rows(t, op):
    r, c = t.shape
    return op(t.reshape(r // SUBLANES, SUBLANES, c), axis=0)


def _attn_kernel(q_ref, k_ref, v_ref, o_ref, sa_scr, sb_scr, ma_scr, mb_scr, acc_scr):
    t = pl.program_id(0)
    q_t = q_ref[0]
    tq = q_t.shape[1]
    n_steps = k_ref.shape[1] // TK_ATTN
    n_sub = TK_ATTN // TK_SUB
    tq_part = tq // n_steps

    def scores(s_new, j, m_run, tk):
        start = pl.multiple_of(j * tk, tk)
        s_t = _dot(k_ref[0, pl.ds(start, tk), :], q_t)
        s_new[pl.ds(start, tk), :] = s_t
        return jnp.maximum(m_run, _fold_rows(s_t, jnp.max))

    def values(s_old, j, m, acc, tk):
        start = pl.multiple_of(j * tk, tk)
        p_t = jnp.exp2(s_old[pl.ds(start, tk), :] - m)
        return acc + _dot(v_ref[0, :, pl.ds(start, tk)], p_t.astype(BF16))

    def finish(i):
        cols = pl.ds(pl.multiple_of(i * tq_part, tq_part), tq_part)
        out = (acc_scr[0:V_DIM, cols] / acc_scr[V_DIM:V_DIM + 1, cols]).T
        o_ref[0, cols, :] = out.astype(BF16)

    m_init = jnp.full((SUBLANES, tq), -jnp.inf, F32)
    acc_init = jnp.zeros((V_EXT, tq), F32)

    @pl.when(t == 0)
    def _():
        ma_scr[...] = lax.fori_loop(
            0, n_steps, lambda i, m_run: scores(sa_scr, i, m_run, TK_ATTN), m_init)
        acc_scr[...] = jnp.ones_like(acc_scr)

    def steady(s_new, m_new, s_old, m_old):
        m = jnp.max(m_old[...], axis=0, keepdims=True)

        def body(i, carry):
            m_run, acc = carry
            for u in range(n_sub):
                m_run = scores(s_new, i * n_sub + u, m_run, TK_SUB)
                acc = values(s_old, i * n_sub + u, m, acc, TK_SUB)
                if u == n_sub // 2:
                    finish(i)
            return m_run, acc

        m_run, acc = lax.fori_loop(0, n_steps, body, (m_init, acc_init))
        m_new[...] = m_run
        acc_scr[...] = acc

    odd = lax.rem(t, 2) == 1
    pl.when(odd)(lambda: steady(sb_scr, mb_scr, sa_scr, ma_scr))
    pl.when(jnp.logical_and(t > 0, jnp.logical_not(odd)))(
        lambda: steady(sa_scr, ma_scr, sb_scr, mb_scr))


def _attention(q_t, k, v_t):
    b, nh, _, s = q_t.shape
    n_q = s // TQ
    n_items = b * nh * n_q
    item = lambda t, lag: jnp.clip(t - lag, 0, n_items - 1)
    return pl.pallas_call(
        _attn_kernel,
        out_shape=jax.ShapeDtypeStruct((b, s, nh * V_DIM), BF16),
        grid=(n_items + 2,),
        in_specs=[
            pl.BlockSpec((1, HEAD_PAD, TQ), lambda t: (item(t, 0) // n_q, 0, item(t, 0) % n_q)),
            pl.BlockSpec((1, s, HEAD_PAD), lambda t: (item(t, 0) // n_q, 0, 0)),
            pl.BlockSpec((1, V_EXT, s), lambda t: (item(t, 1) // n_q, 0, 0)),
        ],
        out_specs=pl.BlockSpec(
            (1, TQ, V_DIM),
            lambda t: (item(t, 2) // (nh * n_q), item(t, 2) % n_q, (item(t, 2) // n_q) % nh)),
        scratch_shapes=[pltpu.VMEM((s, TQ), F32), pltpu.VMEM((s, TQ), F32),
                        pltpu.VMEM((SUBLANES, TQ), F32), pltpu.VMEM((SUBLANES, TQ), F32),
                        pltpu.VMEM((V_EXT, TQ), F32)],
        compiler_params=pltpu.CompilerParams(
            dimension_semantics=("arbitrary",), vmem_limit_bytes=VMEM_LIMIT),
        name="attention",
    )(q_t.reshape(b * nh, HEAD_PAD, s), k.reshape(b * nh, s, HEAD_PAD),
      v_t.reshape(b * nh, V_EXT, s))


def _mix_out_kernel(x_ref, mod_ref, yf_ref, yb_ref, z_ref, at_ref, sg_ref, ag_ref,
                    wo_s_ref, wo_a_ref, o_ref):
    gate = mod_ref[0, 5:6, :]
    gw = SSD_INNER // SSD_GROUPS
    for r0 in range(0, x_ref.shape[1], ROWS_OUT):
        rows = slice(r0, r0 + ROWS_OUT)
        attn = _rms(at_ref[0, rows, :].astype(F32), ag_ref[...]).astype(BF16)
        mixed = _dot(attn, wo_a_ref[...])
        for i in range(SSD_GROUPS):
            cols = slice(i * gw, (i + 1) * gw)
            z = z_ref[0, rows, cols].astype(F32)
            y = (yf_ref[0, rows, cols].astype(F32) + yb_ref[0, rows, cols].astype(F32)) \
                * (z * _sigmoid(z))
            mixed = mixed + _dot(_rms(y, sg_ref[:, cols]).astype(BF16), wo_s_ref[cols, :])
        o_ref[0, rows, :] = x_ref[0, rows, :] + gate * mixed


def _mix_out(x, mod, y_f, y_b, z, attn, sg, ag, wo_s, wo_a):
    b, s, d = x.shape
    tile = pl.BlockSpec((1, TM_OUT, d), lambda i, j: (i, j, 0))
    return pl.pallas_call(
        _mix_out_kernel,
        out_shape=jax.ShapeDtypeStruct(x.shape, F32),
        grid=(b, s // TM_OUT),
        in_specs=[tile, pl.BlockSpec((1, N_MOD, d), lambda i, j: (i, 0, 0)),
                  tile, tile, tile, tile,
                  _const_spec((1, d)), _const_spec((1, d)),
                  _const_spec(wo_s.shape), _const_spec(wo_a.shape)],
        out_specs=tile,
        compiler_params=pltpu.CompilerParams(
            dimension_semantics=("arbitrary", "arbitrary"), vmem_limit_bytes=VMEM_LIMIT),
        name="mix_out",
    )(x, mod, y_f, y_b, z, attn, sg, ag, wo_s, wo_a)


def _pad_cols(w, width):
    return jnp.pad(w, ((0, 0), (0, width - w.shape[1])))


def _prep_w_in(w_in):
    cuts = np.cumsum([SSD_INNER, CONV_CH, 2 * SSD_HEADS, Q_LORA, KV_LORA])
    w_z, w_xbc, w_dt, w_q, w_kv, w_kr = jnp.split(w_in, [int(c) for c in cuts], axis=1)
    t1, t2 = w_kr[:, :QK_ROPE // 2], w_kr[:, QK_ROPE // 2:]
    kra = _pad_cols(jnp.concatenate([t1, t2], axis=1), LANES)
    krb = _pad_cols(jnp.concatenate([-t2, t1], axis=1), LANES)
    return jnp.concatenate([w_z, w_xbc, _pad_cols(w_dt, LANES), w_q, w_kv, kra, krb],
                           axis=1).astype(BF16)


def _prep_w_q(w_q_b):
    return w_q_b.T.astype(BF16)


def _prep_w_kv(w_kv_b):
    w = w_kv_b.reshape(KV_LORA, MLA_HEADS, QK_NOPE + V_DIM)
    w_kn = w[:, :, :QK_NOPE].reshape(KV_LORA, MLA_HEADS * QK_NOPE).astype(BF16)
    w_v_t = w[:, :, QK_NOPE:].reshape(KV_LORA, MLA_HEADS * V_DIM).T.astype(BF16)
    return w_kn, w_v_t


def kernel(x, c, positions, ada_w, ada_b, norm_g, ffn_w_gate, ffn_w_up, ffn_w_down, w_in, conv_w,
           conv_b, dt_bias, a_log, d_skip, ssd_norm_g, q_norm_g, w_q_b, kv_norm_g, w_kv_b,
           attn_norm_g, w_out, final_norm_g):
    b, s, d = x.shape
    depth = ada_w.shape[0]
    assert s % TM_FFN == 0 and s % TM_MIX == 0 and s % TM_OUT == 0 and s % SSD_CHUNK == 0
    assert s % TQ == 0 and s % TK_ATTN == 0 and s % ROPE_TS == 0

    c_pad = jnp.pad(c, ((0, SUBLANES - b), (0, 0)))
    mod_all = _ada(c_pad, ada_w, ada_b)
    tabs = _rope_tables(positions)
    fg = final_norm_g.reshape(1, d)

    for l in range(depth):
        mod = mod_all[l, :b].reshape(b, N_MOD, d)
        bf = lambda w: w.astype(BF16)
        x = _ffn(x, mod, norm_g[l, 0].reshape(1, d), bf(ffn_w_gate[l, 0]), bf(ffn_w_up[l, 0]),
                 bf(ffn_w_down[l, 0]), fg, row0=0, final=False)

        w_kn, w_v_t = _prep_w_kv(w_kv_b[l])
        dtb = _pad_cols(dt_bias[l].reshape(1, 2 * SSD_HEADS), LANES)
        z, xbc, dt, q_t, k, v_t = _mix_in(
            x, mod, norm_g[l, 1].reshape(1, d), _prep_w_in(w_in[l]), conv_w[l],
            conv_b[l].reshape(1, CONV_CH), dtb, q_norm_g[l].reshape(1, Q_LORA), _prep_w_q(w_q_b[l]),
            kv_norm_g[l].reshape(1, KV_LORA), w_kn, w_v_t, tabs)

        alog_pad = _pad_cols(a_log[l].reshape(1, 2 * SSD_HEADS), LANES)
        d_exp = jnp.repeat(d_skip[l], SSD_HEADDIM).reshape(1, SSD_INNER)
        y_f, y_b = _ssd(xbc, dt, alog_pad, d_exp)
        attn = _attention(q_t, k, v_t)

        x = _mix_out(x, mod, y_f, y_b, z, attn, ssd_norm_g[l].reshape(1, SSD_INNER),
                     attn_norm_g[l].reshape(1, MLA_HEADS * V_DIM),
                     bf(w_out[l, :SSD_INNER]), bf(w_out[l, SSD_INNER:]))

        x = _ffn(x, mod, norm_g[l, 2].reshape(1, d), bf(ffn_w_gate[l, 1]), bf(ffn_w_up[l, 1]),
                 bf(ffn_w_down[l, 1]), fg, row0=6, final=(l == depth - 1))
    return x
```

```python
import functools
import math

import jax
import jax.numpy as jnp
import numpy as np
from jax import lax
from jax.experimental import pallas as pl
from jax.experimental.pallas import tpu as pltpu

F32 = jnp.float32
BF16 = jnp.bfloat16

D_MODEL = 1024
D_FF = 2816
N_MOD = 9
SSD_INNER = 1024
SSD_HEADDIM = 64
SSD_HEADS = 16
SSD_GROUPS = 2
SSD_STATE = 128
SSD_CONV = 5
SSD_CHUNK = 128
SSD_CHUNKS_PER_STEP = 4
CONV_CH = SSD_INNER + 2 * SSD_GROUPS * SSD_STATE
MLA_HEADS = 8
QK_NOPE = 128
QK_ROPE = 64
V_DIM = 128
V_EXT = V_DIM + 16
Q_LORA = 512
KV_LORA = 256
ROPE_THETA = 10000.0
EPS = 1e-6

LANES = 128
SUBLANES = 8
HEAD_PAD = 256
VMEM_LIMIT = 56 * 1024 * 1024

C_Z = 0
C_XBC = C_Z + SSD_INNER
C_DT = C_XBC + CONV_CH
C_Q = C_DT + LANES
C_KV = C_Q + Q_LORA
C_KRA = C_KV + KV_LORA
C_KRB = C_KRA + LANES
IN_COLS_EXT = C_KRB + LANES

TM_FFN = 512
TM_MIX = 256
ROWS_OUT = 256
FF_SPLITS = ((0, 1536), (1536, D_FF))
TQ = 512
TK_ATTN = 4096
TK_SUB = 512
ROPE_TS = 1024
CONV_HALO = SUBLANES
Q_SCALE = (QK_NOPE + QK_ROPE) ** -0.5 * math.log2(math.e)


def _dot(a, b):
    return jnp.dot(a, b, preferred_element_type=F32)


def _dot_nt(a, b):
    return lax.dot_general(a, b, (((1,), (1,)), ((), ())), preferred_element_type=F32)


def _rms(x, g):
    return x * lax.rsqrt(jnp.mean(x * x, axis=-1, keepdims=True) + EPS) * g


def _modulate(x, g, shift, scale):
    return _rms(x, g) * (1.0 + scale) + shift


def _sigmoid(x):
    return 1.0 / (1.0 + jnp.exp(-x))


def _const_spec(shape):
    nd = len(shape)
    return pl.BlockSpec(shape, lambda *_: (0,) * nd, pipeline_mode=pl.Buffered(1))


def _ada_kernel(c_ref, w_ref, b_ref, o_ref):
    c = c_ref[...]
    c_act = (c * _sigmoid(c)).astype(BF16)
    o_ref[0] = _dot(c_act, w_ref[0].astype(BF16)) + b_ref[0]


def _ada(c_pad, ada_w, ada_b):
    depth, d, n = ada_w.shape
    tn = 1536
    return pl.pallas_call(
        _ada_kernel,
        out_shape=jax.ShapeDtypeStruct((depth, SUBLANES, n), F32),
        grid=(depth, n // tn),
        in_specs=[
            pl.BlockSpec((SUBLANES, d), lambda l, j: (0, 0)),
            pl.BlockSpec((1, d, tn), lambda l, j: (l, 0, j)),
            pl.BlockSpec((1, 1, tn), lambda l, j: (l, 0, j)),
        ],
        out_specs=pl.BlockSpec((1, SUBLANES, tn), lambda l, j: (l, 0, j)),
        compiler_params=pltpu.CompilerParams(
            dimension_semantics=("arbitrary", "arbitrary"), vmem_limit_bytes=VMEM_LIMIT),
        name="ada",
    )(c_pad, ada_w, ada_b.reshape(depth, 1, n))


def _rope_kernel(pos_ref, inv_ref, cos_t_ref, sin_t_ref, cos_n_ref, sin_n_ref):
    ang = inv_ref[...] * pos_ref[0].astype(F32)
    c, s = jnp.cos(ang), jnp.sin(ang)
    zero = jnp.zeros((LANES - QK_ROPE, ang.shape[1]), F32)
    c_t = jnp.concatenate([c, c, zero], axis=0)
    s_t = jnp.concatenate([s, s, zero], axis=0)
    cos_t_ref[0] = c_t
    sin_t_ref[0] = s_t
    cos_n_ref[0] = c_t.T
    sin_n_ref[0] = s_t.T


def _rope_tables(positions):
    b, s = positions.shape
    inv_freq = 1.0 / (ROPE_THETA ** (jnp.arange(0, QK_ROPE, 2, dtype=F32) / QK_ROPE))
    t_shape = jax.ShapeDtypeStruct((b, LANES, s), F32)
    n_shape = jax.ShapeDtypeStruct((b, s, LANES), F32)
    t_spec = pl.BlockSpec((1, LANES, ROPE_TS), lambda i, j: (i, 0, j))
    n_spec = pl.BlockSpec((1, ROPE_TS, LANES), lambda i, j: (i, j, 0))
    return pl.pallas_call(
        _rope_kernel,
        out_shape=(t_shape, t_shape, n_shape, n_shape),
        grid=(b, s // ROPE_TS),
        in_specs=[
            pl.BlockSpec((1, 1, ROPE_TS), lambda i, j: (i, 0, j)),
            pl.BlockSpec((QK_ROPE // 2, 1), lambda i, j: (0, 0)),
        ],
        out_specs=(t_spec, t_spec, n_spec, n_spec),
        compiler_params=pltpu.CompilerParams(dimension_semantics=("arbitrary", "arbitrary")),
        name="rope_tables",
    )(positions.reshape(b, 1, s), inv_freq.reshape(QK_ROPE // 2, 1))


def _ffn_tile(x, mod_ref, g_ref, wg_ref, wu_ref, wd_ref, fg_ref, o_ref, rows, row0, final):
    shift = mod_ref[0, row0:row0 + 1, :]
    scale = mod_ref[0, row0 + 1:row0 + 2, :]
    gate = mod_ref[0, row0 + 2:row0 + 3, :]
    h = _modulate(x, g_ref[...], shift, scale).astype(BF16)
    down = None
    for c0, c1 in FF_SPLITS:
        a = _dot(h, wg_ref[:, c0:c1])
        u = _dot(h, wu_ref[:, c0:c1])
        part = _dot((a * _sigmoid(a) * u).astype(BF16), wd_ref[c0:c1, :])
        down = part if down is None else down + part
    out = x + 0.5 * gate * down
    if final:
        out = _rms(out, fg_ref[...])
    o_ref[0, rows, :] = out


def _ffn_kernel(x_ref, mod_ref, g_ref, wg_ref, wu_ref, wd_ref, fg_ref, o_ref, *, row0, final):
    _ffn_tile(x_ref[0], mod_ref, g_ref, wg_ref, wu_ref, wd_ref, fg_ref, o_ref, slice(None),
              row0, final)


def _mix_out_ffn_kernel(x_ref, mod_ref, yf_ref, yb_ref, z_ref, at_ref, sg_ref, ag_ref,
                        wo_s_ref, wo_a_ref, g_ref, wg_ref, wu_ref, wd_ref, fg_ref,
                        o_ref, *, row0, final):
    gate = mod_ref[0, 5:6, :]
    gw = SSD_INNER // SSD_GROUPS
    for r0 in range(0, x_ref.shape[1], ROWS_OUT):
        rows = slice(r0, r0 + ROWS_OUT)
        attn = _rms(at_ref[0, rows, :].astype(F32), ag_ref[...]).astype(BF16)
        mixed = _dot(attn, wo_a_ref[...])
        for i in range(SSD_GROUPS):
            cols = slice(i * gw, (i + 1) * gw)
            z = z_ref[0, rows, cols].astype(F32)
            y = (yf_ref[0, rows, cols].astype(F32) + yb_ref[0, rows, cols].astype(F32)) \
                * (z * _sigmoid(z))
            mixed = mixed + _dot(_rms(y, sg_ref[:, cols]).astype(BF16), wo_s_ref[cols, :])
        _ffn_tile(x_ref[0, rows, :] + gate * mixed, mod_ref, g_ref, wg_ref, wu_ref, wd_ref,
                  fg_ref, o_ref, rows, row0, final)


def _ffn(x, mod, g, wg, wu, wd, fg, *, row0, final, mixer=None):
    b, s, d = x.shape
    tile = pl.BlockSpec((1, TM_FFN, d), lambda i, j: (i, j, 0))
    ffn_args = (g, wg, wu, wd, fg)
    if mixer is None:
        body, name, mix_args = _ffn_kernel, "ffn", ()
    else:
        body, name, mix_args = _mix_out_ffn_kernel, "mix_out_ffn", tuple(mixer)
    mix_specs = [tile] * 4 + [_const_spec(a.shape) for a in mix_args[4:]] if mix_args else []
    return pl.pallas_call(
        functools.partial(body, row0=row0, final=final),
        out_shape=jax.ShapeDtypeStruct(x.shape, F32),
        grid=(b, s // TM_FFN),
        in_specs=[tile, pl.BlockSpec((1, N_MOD, d), lambda i, j: (i, 0, 0))] + mix_specs
        + [_const_spec(a.shape) for a in ffn_args],
        out_specs=tile,
        compiler_params=pltpu.CompilerParams(
            dimension_semantics=("arbitrary", "arbitrary"), vmem_limit_bytes=VMEM_LIMIT),
        name=name,
    )(x, mod, *mix_args, *ffn_args)


def _mix_in_kernel(x_ref, xp_ref, xn_ref, mod_ref, g_ref, win_ref, cw_ref, cb_ref, dtb_ref,
                   qg_ref, wq_ref, kvg_ref, wkn_ref, wv_ref,
                   cos_t_ref, sin_t_ref, cos_n_ref, sin_n_ref,
                   z_ref, xbc_ref, dt_ref, q_ref, k_ref, v_ref, u_scr):
    j = pl.program_id(1)
    nj = pl.num_programs(1)
    tm = x_ref.shape[1]
    g = g_ref[...]
    shift = mod_ref[0, 3:4, :]
    scale = mod_ref[0, 4:5, :]

    h = _modulate(x_ref[0], g, shift, scale).astype(BF16)
    lat = _dot(h, win_ref[:, C_DT:])
    lat_off = lambda c: c - C_DT

    x_halo = jnp.concatenate([xp_ref[0], xn_ref[0]], axis=0)
    h_halo = _modulate(x_halo, g, shift, scale).astype(BF16)
    u_halo = _dot(h_halo, win_ref[:, C_XBC:C_DT])
    u_scr[0:CONV_HALO, :] = jnp.where(j > 0, u_halo[:CONV_HALO], 0.0)
    u_scr[CONV_HALO:CONV_HALO + tm, :] = _dot(h, win_ref[:, C_XBC:C_DT])
    u_scr[CONV_HALO + tm:, :] = jnp.where(j < nj - 1, u_halo[CONV_HALO:], 0.0)

    def conv_block(c):
        cols = slice(c * LANES, (c + 1) * LANES)
        acc = jnp.broadcast_to(cb_ref[:, cols], (tm, LANES))
        for k in range(SSD_CONV):
            off = CONV_HALO - SSD_CONV // 2 + k
            acc = acc + cw_ref[k:k + 1, cols] * u_scr[off:off + tm, cols]
        xbc_ref[0, :, cols] = acc * _sigmoid(acc)

    n_conv = CONV_CH // LANES
    dt_ref[0] = jax.nn.softplus(lat[:, lat_off(C_DT):lat_off(C_Q)] + dtb_ref[...])
    hq = _rms(lat[:, lat_off(C_Q):lat_off(C_KV)], qg_ref[...]).astype(BF16)
    ckv = _rms(lat[:, lat_off(C_KV):lat_off(C_KRA)], kvg_ref[...]).astype(BF16)
    k_rope = (lat[:, lat_off(C_KRA):lat_off(C_KRB)] * cos_n_ref[0]
              + lat[:, lat_off(C_KRB):] * sin_n_ref[0]).astype(BF16)

    half = QK_ROPE // 2
    qk_dim = QK_NOPE + QK_ROPE
    cos_t = cos_t_ref[0, 0:half, :] * Q_SCALE
    sin_t = sin_t_ref[0, 0:half, :] * Q_SCALE
    for hd in range(MLA_HEADS):
        q_h = _dot_nt(wq_ref[hd * qk_dim:(hd + 1) * qk_dim, :], hq)
        v_h = _dot_nt(wv_ref[hd * V_DIM:(hd + 1) * V_DIM, :], ckv)
        if hd % 2 == 0:
            k_pair = _dot(ckv, wkn_ref[:, hd * QK_NOPE:(hd + 2) * QK_NOPE])
        conv_block(hd)
        q_ref[0, hd, 0:LANES, :] = (q_h[0:QK_NOPE] * Q_SCALE).astype(BF16)
        t1 = q_h[QK_NOPE:QK_NOPE + half]
        t2 = q_h[QK_NOPE + half:]
        q_ref[0, hd, LANES:LANES + half, :] = (t1 * cos_t - t2 * sin_t).astype(BF16)
        q_ref[0, hd, LANES + half:LANES + QK_ROPE, :] = (t2 * cos_t + t1 * sin_t).astype(BF16)
        q_ref[0, hd, LANES + QK_ROPE:, :] = jnp.zeros((HEAD_PAD - LANES - QK_ROPE, tm), BF16)
        k_ref[0, hd, :, 0:LANES] = k_pair[:, (hd % 2) * QK_NOPE:(hd % 2 + 1) * QK_NOPE].astype(BF16)
        k_ref[0, hd, :, LANES:] = k_rope
        v_ref[0, hd, 0:V_DIM, :] = v_h.astype(BF16)
        v_ref[0, hd, V_DIM:, :] = jnp.ones((V_EXT - V_DIM, tm), BF16)

    z_cols = 2 * LANES
    for c in range(SSD_INNER // z_cols):
        z_ref[0, :, c * z_cols:(c + 1) * z_cols] = _dot(
            h, win_ref[:, C_Z + c * z_cols:C_Z + (c + 1) * z_cols]).astype(BF16)
        conv_block(MLA_HEADS + c)


def _mix_in(x, mod, g, win, cw, cb, dtb, qg, wq, kvg, wkn, wv, tabs):
    b, s, d = x.shape
    tm = TM_MIX
    cos_t, sin_t, cos_n, sin_n = tabs
    nblk = s // CONV_HALO
    per = tm // CONV_HALO
    out_shape = (
        jax.ShapeDtypeStruct((b, s, SSD_INNER), BF16),
        jax.ShapeDtypeStruct((b, s, CONV_CH), F32),
        jax.ShapeDtypeStruct((b, s, LANES), F32),
        jax.ShapeDtypeStruct((b, MLA_HEADS, HEAD_PAD, s), BF16),
        jax.ShapeDtypeStruct((b, MLA_HEADS, s, HEAD_PAD), BF16),
        jax.ShapeDtypeStruct((b, MLA_HEADS, V_EXT, s), BF16),
    )
    tok = lambda w: pl.BlockSpec((1, tm, w), lambda i, j: (i, j, 0))
    in_specs = [
        tok(d),
        pl.BlockSpec((1, CONV_HALO, d), lambda i, j: (i, jnp.maximum(j * per - 1, 0), 0)),
        pl.BlockSpec((1, CONV_HALO, d), lambda i, j: (i, jnp.minimum((j + 1) * per, nblk - 1), 0)),
        pl.BlockSpec((1, N_MOD, d), lambda i, j: (i, 0, 0)),
        _const_spec((1, d)),
        _const_spec(win.shape),
        _const_spec(cw.shape),
        _const_spec(cb.shape),
        _const_spec(dtb.shape),
        _const_spec(qg.shape),
        _const_spec(wq.shape),
        _const_spec(kvg.shape),
        _const_spec(wkn.shape),
        _const_spec(wv.shape),
        pl.BlockSpec((1, LANES, tm), lambda i, j: (i, 0, j)),
        pl.BlockSpec((1, LANES, tm), lambda i, j: (i, 0, j)),
        tok(LANES),
        tok(LANES),
    ]
    out_specs = (
        tok(SSD_INNER),
        tok(CONV_CH),
        tok(LANES),
        pl.BlockSpec((1, MLA_HEADS, HEAD_PAD, tm), lambda i, j: (i, 0, 0, j)),
        pl.BlockSpec((1, MLA_HEADS, tm, HEAD_PAD), lambda i, j: (i, 0, j, 0)),
        pl.BlockSpec((1, MLA_HEADS, V_EXT, tm), lambda i, j: (i, 0, 0, j)),
    )
    return pl.pallas_call(
        _mix_in_kernel,
        out_shape=out_shape,
        grid=(b, s // tm),
        in_specs=in_specs,
        out_specs=out_specs,
        scratch_shapes=[pltpu.VMEM((tm + 2 * CONV_HALO, CONV_CH), F32)],
        compiler_params=pltpu.CompilerParams(
            dimension_semantics=("arbitrary", "arbitrary"), vmem_limit_bytes=VMEM_LIMIT),
        name="mix_in",
    )(x, x, x, mod, g, win, cw, cb, dtb, qg, wq, kvg, wkn, wv, cos_t, sin_t, cos_n, sin_n)


def _split3(x):
    hi = x.astype(BF16)
    r1 = x - hi.astype(F32)
    mid = r1.astype(BF16)
    lo = (r1 - mid.astype(F32)).astype(BF16)
    return hi, mid, lo


def _ssd_direction(xbc, dt, a_row, state_ref, y_ref, y_rows, d_row, *, backward):
    L = SSD_CHUNK
    pair_w = 2 * SSD_HEADDIM
    pairs_per_group = SSD_HEADS // SSD_GROUPS // 2
    off = SSD_HEADS if backward else 0
    edge = 0 if backward else L - 1
    row_i = lax.broadcasted_iota(jnp.int32, (L, L), 0)
    col_i = lax.broadcasted_iota(jnp.int32, (L, L), 1)
    keep = (row_i <= col_i) if backward else (row_i >= col_i)
    left = col_i < SSD_HEADDIM
    tri = jnp.where(keep, 1.0, 0.0).astype(BF16)
    a_dt = dt * a_row
    hi, mid, lo = _split3(a_dt)
    a_cs = _dot(tri, hi) + _dot(tri, mid) + _dot(tri, lo)
    a_cs_t = a_cs.T
    src_t = (a_cs - jnp.log2(dt)).T
    yield
    c_off = SSD_INNER + SSD_GROUPS * SSD_STATE
    for grp in range(SSD_GROUPS):
        b_g = xbc[:, SSD_INNER + grp * SSD_STATE:SSD_INNER + (grp + 1) * SSD_STATE]
        c_bf = xbc[:, c_off + grp * SSD_STATE:c_off + (grp + 1) * SSD_STATE].astype(BF16)
        cb = _dot_nt(c_bf, b_g.astype(BF16))
        b_t = b_g.T
        states = [state_ref[grp * pairs_per_group + j] for j in range(pairs_per_group)]
        y_off = _dot(c_bf, jnp.concatenate(states, axis=1).astype(BF16))
        yield
        for j in range(pairs_per_group):
            pr = grp * pairs_per_group + j
            xs_pair = xbc[:, pr * pair_w:(pr + 1) * pair_w]
            rhs = jnp.concatenate([jnp.where(left, xs_pair, 0.0).astype(BF16),
                                   jnp.where(left, 0.0, xs_pair).astype(BF16)], axis=0)
            m_parts, w_parts, col_parts, tot_parts = [], [], [], []
            for hd in (2 * pr, 2 * pr + 1):
                col = jnp.broadcast_to(a_cs[:, off + hd:off + hd + 1], (L, L))
                src = src_t[off + hd:off + hd + 1, :]
                decay_dt = jnp.exp2(jnp.where(keep, col - src, -jnp.inf))
                m_parts.append((cb * decay_dt).astype(BF16))
                a_tot = a_cs_t[off + hd:off + hd + 1, edge:edge + 1]
                w_parts.append((b_t * jnp.exp2(a_tot - src)).astype(BF16))
                col_parts.append(col)
                tot_parts.append(a_tot)
            y = _dot(jnp.concatenate(m_parts, axis=1), rhs)
            upd = _dot(jnp.concatenate(w_parts, axis=1), rhs)
            yield
            y = y + jnp.exp2(jnp.where(left, col_parts[0], col_parts[1])) \
                * y_off[:, j * pair_w:(j + 1) * pair_w]
            if d_row is not None:
                y = y + d_row[:, pr * pair_w:(pr + 1) * pair_w] * xs_pair
            y_ref[0, y_rows, pr * pair_w:(pr + 1) * pair_w] = y.astype(BF16)
            keep_frac = jnp.exp2(jnp.where(left[0:1], tot_parts[0], tot_parts[1]))
            state_ref[pr] = keep_frac * states[j] + upd
            yield


def _ssd_kernel(xf_ref, xb_ref, dtf_ref, dtb_ref, alog_ref, d_ref, yf_ref, yb_ref, sf_ref, sb_ref):
    @pl.when(pl.program_id(1) == 0)
    def _():
        sf_ref[...] = jnp.zeros_like(sf_ref)
        sb_ref[...] = jnp.zeros_like(sb_ref)

    a_row = -jnp.exp(alog_ref[...]) * math.log2(math.e)
    for k in range(SSD_CHUNKS_PER_STEP):
        rf = slice(k * SSD_CHUNK, (k + 1) * SSD_CHUNK)
        rb = slice((SSD_CHUNKS_PER_STEP - 1 - k) * SSD_CHUNK, (SSD_CHUNKS_PER_STEP - k) * SSD_CHUNK)
        fwd = _ssd_direction(xf_ref[0, rf, :], dtf_ref[0, rf, :], a_row, sf_ref, yf_ref, rf,
                             d_ref[...], backward=False)
        bwd = _ssd_direction(xb_ref[0, rb, :], dtb_ref[0, rb, :], a_row, sb_ref, yb_ref, rb,
                             None, backward=True)
        for _ in zip(fwd, bwd):
            pass


def _ssd(xbc, dt, alog_pad, d_exp):
    b, s, _ = xbc.shape
    rows = SSD_CHUNKS_PER_STEP * SSD_CHUNK
    nc = s // rows
    fwd = lambda w: pl.BlockSpec((1, rows, w), lambda i, c: (i, c, 0))
    bwd = lambda w: pl.BlockSpec((1, rows, w), lambda i, c: (i, nc - 1 - c, 0))
    y_shape = jax.ShapeDtypeStruct((b, s, SSD_INNER), BF16)
    state = pltpu.VMEM((SSD_HEADS // 2, SSD_STATE, 2 * SSD_HEADDIM), F32)
    return pl.pallas_call(
        _ssd_kernel,
        out_shape=(y_shape, y_shape),
        grid=(b, nc),
        in_specs=[fwd(CONV_CH), bwd(CONV_CH), fwd(LANES), bwd(LANES),
                  _const_spec((1, LANES)), _const_spec((1, SSD_INNER))],
        out_specs=(fwd(SSD_INNER), bwd(SSD_INNER)),
        scratch_shapes=[state, state],
        compiler_params=pltpu.CompilerParams(
            dimension_semantics=("arbitrary", "arbitrary"), vmem_limit_bytes=VMEM_LIMIT),
        name="ssd",
    )(xbc, xbc, dt, dt, alog_pad, d_exp)


def _fold_rows(t, op):
    r, c = t.shape
    return op(t.reshape(r // SUBLANES, SUBLANES, c), axis=0)


def _attn_kernel(q_ref, k_ref, v_ref, o_ref, sa_scr, sb_scr, ma_scr, mb_scr, acc_scr):
    t = pl.program_id(0)
    q_t = q_ref[0]
    tq = q_t.shape[1]
    n_steps = k_ref.shape[1] // TK_ATTN
    n_sub = TK_ATTN // TK_SUB
    tq_part = tq // n_steps

    def scores(s_new, j, m_run, tk):
        start = pl.multiple_of(j * tk, tk)
        s_t = _dot(k_ref[0, pl.ds(start, tk), :], q_t)
        s_new[pl.ds(start, tk), :] = s_t
        return jnp.maximum(m_run, _fold_rows(s_t, jnp.max))

    def values(s_old, j, m, acc, tk):
        start = pl.multiple_of(j * tk, tk)
        p_t = jnp.exp2(s_old[pl.ds(start, tk), :] - m)
        return acc + _dot(v_ref[0, :, pl.ds(start, tk)], p_t.astype(BF16))

    def finish(i):
        cols = pl.ds(pl.multiple_of(i * tq_part, tq_part), tq_part)
        out = (acc_scr[0:V_DIM, cols] / acc_scr[V_DIM:V_DIM + 1, cols]).T
        o_ref[0, cols, :] = out.astype(BF16)

    m_init = jnp.full((SUBLANES, tq), -jnp.inf, F32)
    acc_init = jnp.zeros((V_EXT, tq), F32)

    @pl.when(t == 0)
    def _():
        ma_scr[...] = lax.fori_loop(
            0, n_steps, lambda i, m_run: scores(sa_scr, i, m_run, TK_ATTN), m_init)
        acc_scr[...] = jnp.ones_like(acc_scr)

    def steady(s_new, m_new, s_old, m_old):
        m = jnp.max(m_old[...], axis=0, keepdims=True)

        def body(i, carry):
            m_run, acc = carry
            for u in range(n_sub):
                m_run = scores(s_new, i * n_sub + u, m_run, TK_SUB)
                acc = values(s_old, i * n_sub + u, m, acc, TK_SUB)
                if u == n_sub // 2:
                    finish(i)
            return m_run, acc

        m_run, acc = lax.fori_loop(0, n_steps, body, (m_init, acc_init))
        m_new[...] = m_run
        acc_scr[...] = acc

    odd = lax.rem(t, 2) == 1
    pl.when(odd)(lambda: steady(sb_scr, mb_scr, sa_scr, ma_scr))
    pl.when(jnp.logical_and(t > 0, jnp.logical_not(odd)))(
        lambda: steady(sa_scr, ma_scr, sb_scr, mb_scr))


def _attention(q_t, k, v_t):
    b, nh, _, s = q_t.shape
    n_q = s // TQ
    n_items = b * nh * n_q
    item = lambda t, lag: jnp.clip(t - lag, 0, n_items - 1)
    return pl.pallas_call(
        _attn_kernel,
        out_shape=jax.ShapeDtypeStruct((b, s, nh * V_DIM), BF16),
        grid=(n_items + 2,),
        in_specs=[
            pl.BlockSpec((1, HEAD_PAD, TQ), lambda t: (item(t, 0) // n_q, 0, item(t, 0) % n_q)),
            pl.BlockSpec((1, s, HEAD_PAD), lambda t: (item(t, 0) // n_q, 0, 0)),
            pl.BlockSpec((1, V_EXT, s), lambda t: (item(t, 1) // n_q, 0, 0)),
        ],
        out_specs=pl.BlockSpec(
            (1, TQ, V_DIM),
            lambda t: (item(t, 2) // (nh * n_q), item(t, 2) % n_q, (item(t, 2) // n_q) % nh)),
        scratch_shapes=[pltpu.VMEM((s, TQ), F32), pltpu.VMEM((s, TQ), F32),
                        pltpu.VMEM((SUBLANES, TQ), F32), pltpu.VMEM((SUBLANES, TQ), F32),
                        pltpu.VMEM((V_EXT, TQ), F32)],
        compiler_params=pltpu.CompilerParams(
            dimension_semantics=("arbitrary",), vmem_limit_bytes=VMEM_LIMIT),
        name="attention",
    )(q_t.reshape(b * nh, HEAD_PAD, s), k.reshape(b * nh, s, HEAD_PAD),
      v_t.reshape(b * nh, V_EXT, s))


def _pad_cols(w, width):
    return jnp.pad(w, ((0, 0), (0, width - w.shape[1])))


def _prep_w_in(w_in):
    cuts = np.cumsum([SSD_INNER, CONV_CH, 2 * SSD_HEADS, Q_LORA, KV_LORA])
    w_z, w_xbc, w_dt, w_q, w_kv, w_kr = jnp.split(w_in, [int(c) for c in cuts], axis=1)
    t1, t2 = w_kr[:, :QK_ROPE // 2], w_kr[:, QK_ROPE // 2:]
    kra = _pad_cols(jnp.concatenate([t1, t2], axis=1), LANES)
    krb = _pad_cols(jnp.concatenate([-t2, t1], axis=1), LANES)
    return jnp.concatenate([w_z, w_xbc, _pad_cols(w_dt, LANES), w_q, w_kv, kra, krb],
                           axis=1).astype(BF16)


def _prep_w_q(w_q_b):
    return w_q_b.T.astype(BF16)


def _prep_w_kv(w_kv_b):
    w = w_kv_b.reshape(KV_LORA, MLA_HEADS, QK_NOPE + V_DIM)
    w_kn = w[:, :, :QK_NOPE].reshape(KV_LORA, MLA_HEADS * QK_NOPE).astype(BF16)
    w_v_t = w[:, :, QK_NOPE:].reshape(KV_LORA, MLA_HEADS * V_DIM).T.astype(BF16)
    return w_kn, w_v_t


def kernel(x, c, positions, ada_w, ada_b, norm_g, ffn_w_gate, ffn_w_up, ffn_w_down, w_in, conv_w,
           conv_b, dt_bias, a_log, d_skip, ssd_norm_g, q_norm_g, w_q_b, kv_norm_g, w_kv_b,
           attn_norm_g, w_out, final_norm_g):
    b, s, d = x.shape
    depth = ada_w.shape[0]
    assert s % TM_FFN == 0 and TM_FFN % ROWS_OUT == 0 and s % TM_MIX == 0
    assert s % (SSD_CHUNK * SSD_CHUNKS_PER_STEP) == 0
    assert s % TQ == 0 and s % TK_ATTN == 0 and s % ROPE_TS == 0

    c_pad = jnp.pad(c, ((0, SUBLANES - b), (0, 0)))
    mod_all = _ada(c_pad, ada_w, ada_b)
    tabs = _rope_tables(positions)
    fg = final_norm_g.reshape(1, d)

    for l in range(depth):
        mod = mod_all[l, :b].reshape(b, N_MOD, d)
        bf = lambda w: w.astype(BF16)
        x = _ffn(x, mod, norm_g[l, 0].reshape(1, d), bf(ffn_w_gate[l, 0]), bf(ffn_w_up[l, 0]),
                 bf(ffn_w_down[l, 0]), fg, row0=0, final=False)

        w_kn, w_v_t = _prep_w_kv(w_kv_b[l])
        dtb = _pad_cols(dt_bias[l].reshape(1, 2 * SSD_HEADS), LANES)
        z, xbc, dt, q_t, k, v_t = _mix_in(
            x, mod, norm_g[l, 1].reshape(1, d), _prep_w_in(w_in[l]), conv_w[l],
            conv_b[l].reshape(1, CONV_CH), dtb, q_norm_g[l].reshape(1, Q_LORA), _prep_w_q(w_q_b[l]),
            kv_norm_g[l].reshape(1, KV_LORA), w_kn, w_v_t, tabs)

        alog_pad = _pad_cols(a_log[l].reshape(1, 2 * SSD_HEADS), LANES)
        d_exp = jnp.repeat(d_skip[l], SSD_HEADDIM).reshape(1, SSD_INNER)
        y_f, y_b = _ssd(xbc, dt, alog_pad, d_exp)
        attn = _attention(q_t, k, v_t)

        mixer = (y_f, y_b, z, attn, ssd_norm_g[l].reshape(1, SSD_INNER),
                 attn_norm_g[l].reshape(1, MLA_HEADS * V_DIM),
                 bf(w_out[l, :SSD_INNER]), bf(w_out[l, SSD_INNER:]))
        x = _ffn(x, mod, norm_g[l, 2].reshape(1, d), bf(ffn_w_gate[l, 1]), bf(ffn_w_up[l, 1]),
                 bf(ffn_w_down[l, 1]), fg, row0=6, final=(l == depth - 1), mixer=mixer)
    return x
```

```python
import functools
import math

import jax
import jax.numpy as jnp
import numpy as np
from jax import lax
from jax.experimental import pallas as pl
from jax.experimental.pallas import tpu as pltpu

F32 = jnp.float32
BF16 = jnp.bfloat16

D_MODEL = 1024
D_FF = 2816
N_MOD = 9
SSD_INNER = 1024
SSD_HEADDIM = 64
SSD_HEADS = 16
SSD_GROUPS = 2
SSD_STATE = 128
SSD_CONV = 5
SSD_CHUNK = 128
SSD_CHUNKS_PER_STEP = 4
CONV_CH = SSD_INNER + 2 * SSD_GROUPS * SSD_STATE
MLA_HEADS = 8
QK_NOPE = 128
QK_ROPE = 64
V_DIM = 128
V_EXT = V_DIM + 16
Q_LORA = 512
KV_LORA = 256
ROPE_THETA = 10000.0
EPS = 1e-6

LANES = 128
SUBLANES = 8
HEAD_PAD = 256
VMEM_LIMIT = 56 * 1024 * 1024

C_Z = 0
C_XBC = C_Z + SSD_INNER
C_DT = C_XBC + CONV_CH
C_Q = C_DT + LANES
C_KV = C_Q + Q_LORA
C_KRA = C_KV + KV_LORA
C_KRB = C_KRA + LANES
IN_COLS_EXT = C_KRB + LANES

TM_FFN = 512
TM_MIX = 256
ROWS_OUT = 256
FF_SPLITS = ((0, 1536), (1536, D_FF))
TQ = 512
TK_ATTN = 8192
TK_SUB = 512
ROPE_TS = 1024
CONV_HALO = SUBLANES
Q_SCALE = (QK_NOPE + QK_ROPE) ** -0.5 * math.log2(math.e)


def _dot(a, b):
    return jnp.dot(a, b, preferred_element_type=F32)


def _dot_nt(a, b):
    return lax.dot_general(a, b, (((1,), (1,)), ((), ())), preferred_element_type=F32)


def _rms(x, g):
    return x * lax.rsqrt(jnp.mean(x * x, axis=-1, keepdims=True) + EPS) * g


def _modulate(x, g, shift, scale):
    return _rms(x, g) * (1.0 + scale) + shift


def _sigmoid(x):
    return 1.0 / (1.0 + jnp.exp(-x))


def _const_spec(shape):
    nd = len(shape)
    return pl.BlockSpec(shape, lambda *_: (0,) * nd, pipeline_mode=pl.Buffered(1))


def _ada_kernel(c_ref, w_ref, b_ref, o_ref):
    c = c_ref[...]
    c_act = (c * _sigmoid(c)).astype(BF16)
    o_ref[0] = _dot(c_act, w_ref[0].astype(BF16)) + b_ref[0]


def _ada(c_pad, ada_w, ada_b):
    depth, d, n = ada_w.shape
    tn = 1536
    return pl.pallas_call(
        _ada_kernel,
        out_shape=jax.ShapeDtypeStruct((depth, SUBLANES, n), F32),
        grid=(depth, n // tn),
        in_specs=[
            pl.BlockSpec((SUBLANES, d), lambda l, j: (0, 0)),
            pl.BlockSpec((1, d, tn), lambda l, j: (l, 0, j)),
            pl.BlockSpec((1, 1, tn), lambda l, j: (l, 0, j)),
        ],
        out_specs=pl.BlockSpec((1, SUBLANES, tn), lambda l, j: (l, 0, j)),
        compiler_params=pltpu.CompilerParams(
            dimension_semantics=("arbitrary", "arbitrary"), vmem_limit_bytes=VMEM_LIMIT),
        name="ada",
    )(c_pad, ada_w, ada_b.reshape(depth, 1, n))


def _rope_kernel(pos_ref, inv_ref, cos_t_ref, sin_t_ref, cos_n_ref, sin_n_ref):
    ang = inv_ref[...] * pos_ref[0].astype(F32)
    c, s = jnp.cos(ang), jnp.sin(ang)
    zero = jnp.zeros((LANES - QK_ROPE, ang.shape[1]), F32)
    c_t = jnp.concatenate([c, c, zero], axis=0)
    s_t = jnp.concatenate([s, s, zero], axis=0)
    cos_t_ref[0] = c_t
    sin_t_ref[0] = s_t
    cos_n_ref[0] = c_t.T
    sin_n_ref[0] = s_t.T


def _rope_tables(positions):
    b, s = positions.shape
    inv_freq = 1.0 / (ROPE_THETA ** (jnp.arange(0, QK_ROPE, 2, dtype=F32) / QK_ROPE))
    t_shape = jax.ShapeDtypeStruct((b, LANES, s), F32)
    n_shape = jax.ShapeDtypeStruct((b, s, LANES), F32)
    t_spec = pl.BlockSpec((1, LANES, ROPE_TS), lambda i, j: (i, 0, j))
    n_spec = pl.BlockSpec((1, ROPE_TS, LANES), lambda i, j: (i, j, 0))
    return pl.pallas_call(
        _rope_kernel,
        out_shape=(t_shape, t_shape, n_shape, n_shape),
        grid=(b, s // ROPE_TS),
        in_specs=[
            pl.BlockSpec((1, 1, ROPE_TS), lambda i, j: (i, 0, j)),
            pl.BlockSpec((QK_ROPE // 2, 1), lambda i, j: (0, 0)),
        ],
        out_specs=(t_spec, t_spec, n_spec, n_spec),
        compiler_params=pltpu.CompilerParams(dimension_semantics=("arbitrary", "arbitrary")),
        name="rope_tables",
    )(positions.reshape(b, 1, s), inv_freq.reshape(QK_ROPE // 2, 1))


def _ffn_tile(x, mod_ref, g_ref, wg_ref, wu_ref, wd_ref, fg_ref, o_ref, rows, row0, final):
    shift = mod_ref[0, row0:row0 + 1, :]
    scale = mod_ref[0, row0 + 1:row0 + 2, :]
    gate = mod_ref[0, row0 + 2:row0 + 3, :]
    h = _modulate(x, g_ref[...], shift, scale).astype(BF16)
    down = None
    for c0, c1 in FF_SPLITS:
        a = _dot(h, wg_ref[:, c0:c1])
        u = _dot(h, wu_ref[:, c0:c1])
        part = _dot((a * _sigmoid(a) * u).astype(BF16), wd_ref[c0:c1, :])
        down = part if down is None else down + part
    out = x + 0.5 * gate * down
    if final:
        out = _rms(out, fg_ref[...])
    o_ref[0, rows, :] = out


def _ffn_kernel(x_ref, mod_ref, g_ref, wg_ref, wu_ref, wd_ref, fg_ref, o_ref, *, row0, final):
    _ffn_tile(x_ref[0], mod_ref, g_ref, wg_ref, wu_ref, wd_ref, fg_ref, o_ref, slice(None),
              row0, final)


def _mix_out_ffn_kernel(x_ref, mod_ref, yf_ref, yb_ref, z_ref, at_ref, sg_ref, ag_ref,
                        wo_s_ref, wo_a_ref, g_ref, wg_ref, wu_ref, wd_ref, fg_ref,
                        o_ref, *, row0, final):
    gate = mod_ref[0, 5:6, :]
    gw = SSD_INNER // SSD_GROUPS
    for r0 in range(0, x_ref.shape[1], ROWS_OUT):
        rows = slice(r0, r0 + ROWS_OUT)
        attn = _rms(at_ref[0, rows, :].astype(F32), ag_ref[...]).astype(BF16)
        mixed = _dot(attn, wo_a_ref[...])
        for i in range(SSD_GROUPS):
            cols = slice(i * gw, (i + 1) * gw)
            z = z_ref[0, rows, cols].astype(F32)
            y = (yf_ref[0, rows, cols].astype(F32) + yb_ref[0, rows, cols].astype(F32)) \
                * (z * _sigmoid(z))
            mixed = mixed + _dot(_rms(y, sg_ref[:, cols]).astype(BF16), wo_s_ref[cols, :])
        _ffn_tile(x_ref[0, rows, :] + gate * mixed, mod_ref, g_ref, wg_ref, wu_ref, wd_ref,
                  fg_ref, o_ref, rows, row0, final)


def _ffn(x, mod, g, wg, wu, wd, fg, *, row0, final, mixer=None):
    b, s, d = x.shape
    tile = pl.BlockSpec((1, TM_FFN, d), lambda i, j: (i, j, 0))
    ffn_args = (g, wg, wu, wd, fg)
    if mixer is None:
        body, name, mix_args = _ffn_kernel, "ffn", ()
    else:
        body, name, mix_args = _mix_out_ffn_kernel, "mix_out_ffn", tuple(mixer)
    mix_specs = [tile] * 4 + [_const_spec(a.shape) for a in mix_args[4:]] if mix_args else []
    return pl.pallas_call(
        functools.partial(body, row0=row0, final=final),
        out_shape=jax.ShapeDtypeStruct(x.shape, F32),
        grid=(b, s // TM_FFN),
        in_specs=[tile, pl.BlockSpec((1, N_MOD, d), lambda i, j: (i, 0, 0))] + mix_specs
        + [_const_spec(a.shape) for a in ffn_args],
        out_specs=tile,
        compiler_params=pltpu.CompilerParams(
            dimension_semantics=("arbitrary", "arbitrary"), vmem_limit_bytes=VMEM_LIMIT),
        name=name,
    )(x, mod, *mix_args, *ffn_args)


def _mix_in_kernel(x_ref, xp_ref, xn_ref, mod_ref, g_ref, win_ref, cw_ref, cb_ref, dtb_ref,
                   qg_ref, wq_ref, kvg_ref, wkn_ref, wv_ref,
                   cos_t_ref, sin_t_ref, cos_n_ref, sin_n_ref,
                   z_ref, xbc_ref, dt_ref, q_ref, k_ref, v_ref, u_scr):
    j = pl.program_id(1)
    nj = pl.num_programs(1)
    tm = x_ref.shape[1]
    g = g_ref[...]
    shift = mod_ref[0, 3:4, :]
    scale = mod_ref[0, 4:5, :]

    h = _modulate(x_ref[0], g, shift, scale).astype(BF16)
    lat = _dot(h, win_ref[:, C_DT:])
    lat_off = lambda c: c - C_DT

    x_halo = jnp.concatenate([xp_ref[0], xn_ref[0]], axis=0)
    h_halo = _modulate(x_halo, g, shift, scale).astype(BF16)
    u_halo = _dot(h_halo, win_ref[:, C_XBC:C_DT])
    u_scr[0:CONV_HALO, :] = jnp.where(j > 0, u_halo[:CONV_HALO], 0.0)
    u_scr[CONV_HALO:CONV_HALO + tm, :] = _dot(h, win_ref[:, C_XBC:C_DT])
    u_scr[CONV_HALO + tm:, :] = jnp.where(j < nj - 1, u_halo[CONV_HALO:], 0.0)

    def conv_block(c):
        cols = slice(c * LANES, (c + 1) * LANES)
        acc = jnp.broadcast_to(cb_ref[:, cols], (tm, LANES))
        for k in range(SSD_CONV):
            off = CONV_HALO - SSD_CONV // 2 + k
            acc = acc + cw_ref[k:k + 1, cols] * u_scr[off:off + tm, cols]
        xbc_ref[0, :, cols] = acc * _sigmoid(acc)

    n_conv = CONV_CH // LANES
    dt_ref[0] = jax.nn.softplus(lat[:, lat_off(C_DT):lat_off(C_Q)] + dtb_ref[...])
    hq = _rms(lat[:, lat_off(C_Q):lat_off(C_KV)], qg_ref[...]).astype(BF16)
    ckv = _rms(lat[:, lat_off(C_KV):lat_off(C_KRA)], kvg_ref[...]).astype(BF16)
    k_rope = (lat[:, lat_off(C_KRA):lat_off(C_KRB)] * cos_n_ref[0]
              + lat[:, lat_off(C_KRB):] * sin_n_ref[0]).astype(BF16)

    half = QK_ROPE // 2
    qk_dim = QK_NOPE + QK_ROPE
    cos_t = cos_t_ref[0, 0:half, :] * Q_SCALE
    sin_t = sin_t_ref[0, 0:half, :] * Q_SCALE
    for hd in range(MLA_HEADS):
        q_h = _dot_nt(wq_ref[hd * qk_dim:(hd + 1) * qk_dim, :], hq)
        v_h = _dot_nt(wv_ref[hd * V_DIM:(hd + 1) * V_DIM, :], ckv)
        if hd % 2 == 0:
            k_pair = _dot(ckv, wkn_ref[:, hd * QK_NOPE:(hd + 2) * QK_NOPE])
        conv_block(hd)
        q_ref[0, hd, 0:LANES, :] = (q_h[0:QK_NOPE] * Q_SCALE).astype(BF16)
        t1 = q_h[QK_NOPE:QK_NOPE + half]
        t2 = q_h[QK_NOPE + half:]
        q_ref[0, hd, LANES:LANES + half, :] = (t1 * cos_t - t2 * sin_t).astype(BF16)
        q_ref[0, hd, LANES + half:LANES + QK_ROPE, :] = (t2 * cos_t + t1 * sin_t).astype(BF16)
        q_ref[0, hd, LANES + QK_ROPE:, :] = jnp.zeros((HEAD_PAD - LANES - QK_ROPE, tm), BF16)
        k_ref[0, hd, :, 0:LANES] = k_pair[:, (hd % 2) * QK_NOPE:(hd % 2 + 1) * QK_NOPE].astype(BF16)
        k_ref[0, hd, :, LANES:] = k_rope
        v_ref[0, hd, 0:V_DIM, :] = v_h.astype(BF16)
        v_ref[0, hd, V_DIM:, :] = jnp.ones((V_EXT - V_DIM, tm), BF16)

    z_cols = 2 * LANES
    for c in range(SSD_INNER // z_cols):
        z_ref[0, :, c * z_cols:(c + 1) * z_cols] = _dot(
            h, win_ref[:, C_Z + c * z_cols:C_Z + (c + 1) * z_cols]).astype(BF16)
        conv_block(MLA_HEADS + c)


def _mix_in(x, mod, g, win, cw, cb, dtb, qg, wq, kvg, wkn, wv, tabs):
    b, s, d = x.shape
    tm = TM_MIX
    cos_t, sin_t, cos_n, sin_n = tabs
    nblk = s // CONV_HALO
    per = tm // CONV_HALO
    out_shape = (
        jax.ShapeDtypeStruct((b, s, SSD_INNER), BF16),
        jax.ShapeDtypeStruct((b, s, CONV_CH), F32),
        jax.ShapeDtypeStruct((b, s, LANES), F32),
        jax.ShapeDtypeStruct((b, MLA_HEADS, HEAD_PAD, s), BF16),
        jax.ShapeDtypeStruct((b, MLA_HEADS, s, HEAD_PAD), BF16),
        jax.ShapeDtypeStruct((b, MLA_HEADS, V_EXT, s), BF16),
    )
    tok = lambda w: pl.BlockSpec((1, tm, w), lambda i, j: (i, j, 0))
    in_specs = [
        tok(d),
        pl.BlockSpec((1, CONV_HALO, d), lambda i, j: (i, jnp.maximum(j * per - 1, 0), 0)),
        pl.BlockSpec((1, CONV_HALO, d), lambda i, j: (i, jnp.minimum((j + 1) * per, nblk - 1), 0)),
        pl.BlockSpec((1, N_MOD, d), lambda i, j: (i, 0, 0)),
        _const_spec((1, d)),
        _const_spec(win.shape),
        _const_spec(cw.shape),
        _const_spec(cb.shape),
        _const_spec(dtb.shape),
        _const_spec(qg.shape),
        _const_spec(wq.shape),
        _const_spec(kvg.shape),
        _const_spec(wkn.shape),
        _const_spec(wv.shape),
        pl.BlockSpec((1, LANES, tm), lambda i, j: (i, 0, j)),
        pl.BlockSpec((1, LANES, tm), lambda i, j: (i, 0, j)),
        tok(LANES),
        tok(LANES),
    ]
    out_specs = (
        tok(SSD_INNER),
        tok(CONV_CH),
        tok(LANES),
        pl.BlockSpec((1, MLA_HEADS, HEAD_PAD, tm), lambda i, j: (i, 0, 0, j)),
        pl.BlockSpec((1, MLA_HEADS, tm, HEAD_PAD), lambda i, j: (i, 0, j, 0)),
        pl.BlockSpec((1, MLA_HEADS, V_EXT, tm), lambda i, j: (i, 0, 0, j)),
    )
    return pl.pallas_call(
        _mix_in_kernel,
        out_shape=out_shape,
        grid=(b, s // tm),
        in_specs=in_specs,
        out_specs=out_specs,
        scratch_shapes=[pltpu.VMEM((tm + 2 * CONV_HALO, CONV_CH), F32)],
        compiler_params=pltpu.CompilerParams(
            dimension_semantics=("arbitrary", "arbitrary"), vmem_limit_bytes=VMEM_LIMIT),
        name="mix_in",
    )(x, x, x, mod, g, win, cw, cb, dtb, qg, wq, kvg, wkn, wv, cos_t, sin_t, cos_n, sin_n)


def _split3(x):
    hi = x.astype(BF16)
    r1 = x - hi.astype(F32)
    mid = r1.astype(BF16)
    lo = (r1 - mid.astype(F32)).astype(BF16)
    return hi, mid, lo


def _ssd_direction(xbc, dt, a_row, state_ref, y_ref, y_rows, d_row, *, backward):
    L = SSD_CHUNK
    pair_w = 2 * SSD_HEADDIM
    pairs_per_group = SSD_HEADS // SSD_GROUPS // 2
    off = SSD_HEADS if backward else 0
    edge = 0 if backward else L - 1
    row_i = lax.broadcasted_iota(jnp.int32, (L, L), 0)
    col_i = lax.broadcasted_iota(jnp.int32, (L, L), 1)
    keep = (row_i <= col_i) if backward else (row_i >= col_i)
    left = col_i < SSD_HEADDIM
    tri = jnp.where(keep, 1.0, 0.0).astype(BF16)
    a_dt = dt * a_row
    hi, mid, lo = _split3(a_dt)
    a_cs = _dot(tri, hi) + _dot(tri, mid) + _dot(tri, lo)
    a_cs_t = a_cs.T
    src_t = (a_cs - jnp.log2(dt)).T
    yield
    c_off = SSD_INNER + SSD_GROUPS * SSD_STATE
    for grp in range(SSD_GROUPS):
        b_g = xbc[:, SSD_INNER + grp * SSD_STATE:SSD_INNER + (grp + 1) * SSD_STATE]
        c_bf = xbc[:, c_off + grp * SSD_STATE:c_off + (grp + 1) * SSD_STATE].astype(BF16)
        cb = _dot_nt(c_bf, b_g.astype(BF16))
        b_t = b_g.T
        states = [state_ref[grp * pairs_per_group + j] for j in range(pairs_per_group)]
        y_off = _dot(c_bf, jnp.concatenate(states, axis=1).astype(BF16))
        yield
        for j in range(pairs_per_group):
            pr = grp * pairs_per_group + j
            xs_pair = xbc[:, pr * pair_w:(pr + 1) * pair_w]
            rhs = jnp.concatenate([jnp.where(left, xs_pair, 0.0).astype(BF16),
                                   jnp.where(left, 0.0, xs_pair).astype(BF16)], axis=0)
            m_parts, w_parts, col_parts, tot_parts = [], [], [], []
            for hd in (2 * pr, 2 * pr + 1):
                col = jnp.broadcast_to(a_cs[:, off + hd:off + hd + 1], (L, L))
                src = src_t[off + hd:off + hd + 1, :]
                decay_dt = jnp.exp2(jnp.where(keep, col - src, -jnp.inf))
                m_parts.append((cb * decay_dt).astype(BF16))
                a_tot = a_cs_t[off + hd:off + hd + 1, edge:edge + 1]
                w_parts.append((b_t * jnp.exp2(a_tot - src)).astype(BF16))
                col_parts.append(col)
                tot_parts.append(a_tot)
            y = _dot(jnp.concatenate(m_parts, axis=1), rhs)
            upd = _dot(jnp.concatenate(w_parts, axis=1), rhs)
            yield
            y = y + jnp.exp2(jnp.where(left, col_parts[0], col_parts[1])) \
                * y_off[:, j * pair_w:(j + 1) * pair_w]
            if d_row is not None:
                y = y + d_row[:, pr * pair_w:(pr + 1) * pair_w] * xs_pair
            y_ref[0, y_rows, pr * pair_w:(pr + 1) * pair_w] = y.astype(BF16)
            keep_frac = jnp.exp2(jnp.where(left[0:1], tot_parts[0], tot_parts[1]))
            state_ref[pr] = keep_frac * states[j] + upd
            yield


def _ssd_kernel(xf_ref, xb_ref, dtf_ref, dtb_ref, alog_ref, d_ref, yf_ref, yb_ref, sf_ref, sb_ref):
    @pl.when(pl.program_id(1) == 0)
    def _():
        sf_ref[...] = jnp.zeros_like(sf_ref)
        sb_ref[...] = jnp.zeros_like(sb_ref)

    a_row = -jnp.exp(alog_ref[...]) * math.log2(math.e)
    for k in range(SSD_CHUNKS_PER_STEP):
        rf = slice(k * SSD_CHUNK, (k + 1) * SSD_CHUNK)
        rb = slice((SSD_CHUNKS_PER_STEP - 1 - k) * SSD_CHUNK, (SSD_CHUNKS_PER_STEP - k) * SSD_CHUNK)
        fwd = _ssd_direction(xf_ref[0, rf, :], dtf_ref[0, rf, :], a_row, sf_ref, yf_ref, rf,
                             d_ref[...], backward=False)
        bwd = _ssd_direction(xb_ref[0, rb, :], dtb_ref[0, rb, :], a_row, sb_ref, yb_ref, rb,
                             None, backward=True)
        for _ in zip(fwd, bwd):
            pass


def _ssd(xbc, dt, alog_pad, d_exp):
    b, s, _ = xbc.shape
    rows = SSD_CHUNKS_PER_STEP * SSD_CHUNK
    nc = s // rows
    fwd = lambda w: pl.BlockSpec((1, rows, w), lambda i, c: (i, c, 0))
    bwd = lambda w: pl.BlockSpec((1, rows, w), lambda i, c: (i, nc - 1 - c, 0))
    y_shape = jax.ShapeDtypeStruct((b, s, SSD_INNER), BF16)
    state = pltpu.VMEM((SSD_HEADS // 2, SSD_STATE, 2 * SSD_HEADDIM), F32)
    return pl.pallas_call(
        _ssd_kernel,
        out_shape=(y_shape, y_shape),
        grid=(b, nc),
        in_specs=[fwd(CONV_CH), bwd(CONV_CH), fwd(LANES), bwd(LANES),
                  _const_spec((1, LANES)), _const_spec((1, SSD_INNER))],
        out_specs=(fwd(SSD_INNER), bwd(SSD_INNER)),
        scratch_shapes=[state, state],
        compiler_params=pltpu.CompilerParams(
            dimension_semantics=("arbitrary", "arbitrary"), vmem_limit_bytes=VMEM_LIMIT),
        name="ssd",
    )(xbc, xbc, dt, dt, alog_pad, d_exp)


def _fold_rows(t, op):
    r, c = t.shape
    return op(t.reshape(r // SUBLANES, SUBLANES, c), axis=0)


def _attn_kernel(q_ref, k_ref, v_ref, o_ref, sa_scr, sb_scr, ma_scr, mb_scr, acc_scr):
    t = pl.program_id(0)
    q_t = q_ref[0]
    tq = q_t.shape[1]
    n_steps = k_ref.shape[1] // TK_ATTN
    n_sub = TK_ATTN // TK_SUB
    tq_part = tq // n_steps

    def scores(s_new, j, m_run, tk):
        start = pl.multiple_of(j * tk, tk)
        s_t = _dot(k_ref[0, pl.ds(start, tk), :], q_t)
        s_new[pl.ds(start, tk), :] = s_t
        return jnp.maximum(m_run, _fold_rows(s_t, jnp.max))

    def values(s_old, j, m, acc, tk):
        start = pl.multiple_of(j * tk, tk)
        p_t = jnp.exp2(s_old[pl.ds(start, tk), :] - m)
        return acc + _dot(v_ref[0, :, pl.ds(start, tk)], p_t.astype(BF16))

    def finish(i):
        cols = pl.ds(pl.multiple_of(i * tq_part, tq_part), tq_part)
        out = (acc_scr[0:V_DIM, cols] / acc_scr[V_DIM:V_DIM + 1, cols]).T
        o_ref[0, cols, :] = out.astype(BF16)

    m_init = jnp.full((SUBLANES, tq), -jnp.inf, F32)
    acc_init = jnp.zeros((V_EXT, tq), F32)

    @pl.when(t == 0)
    def _():
        ma_scr[...] = lax.fori_loop(
            0, n_steps, lambda i, m_run: scores(sa_scr, i, m_run, TK_ATTN), m_init)
        acc_scr[...] = jnp.ones_like(acc_scr)

    def steady(s_new, m_new, s_old, m_old):
        m = jnp.max(m_old[...], axis=0, keepdims=True)

        def body(i, carry):
            m_run, acc = carry
            for u in range(n_sub):
                acc = values(s_old, i * n_sub + u, m, acc, TK_SUB)
                m_run = scores(s_new, i * n_sub + u, m_run, TK_SUB)
                if u == n_sub // 2:
                    finish(i)
            return m_run, acc

        m_run, acc = lax.fori_loop(0, n_steps, body, (m_init, acc_init))
        m_new[...] = m_run
        acc_scr[...] = acc

    odd = lax.rem(t, 2) == 1
    pl.when(odd)(lambda: steady(sb_scr, mb_scr, sa_scr, ma_scr))
    pl.when(jnp.logical_and(t > 0, jnp.logical_not(odd)))(
        lambda: steady(sa_scr, ma_scr, sb_scr, mb_scr))


def _attention(q_t, k, v_t):
    b, nh, _, s = q_t.shape
    n_q = s // TQ
    n_items = b * nh * n_q
    item = lambda t, lag: jnp.clip(t - lag, 0, n_items - 1)
    return pl.pallas_call(
        _attn_kernel,
        out_shape=jax.ShapeDtypeStruct((b, s, nh * V_DIM), BF16),
        grid=(n_items + 2,),
        in_specs=[
            pl.BlockSpec((1, HEAD_PAD, TQ), lambda t: (item(t, 0) // n_q, 0, item(t, 0) % n_q)),
            pl.BlockSpec((1, s, HEAD_PAD), lambda t: (item(t, 0) // n_q, 0, 0)),
            pl.BlockSpec((1, V_EXT, s), lambda t: (item(t, 1) // n_q, 0, 0)),
        ],
        out_specs=pl.BlockSpec(
            (1, TQ, V_DIM),
            lambda t: (item(t, 2) // (nh * n_q), item(t, 2) % n_q, (item(t, 2) // n_q) % nh)),
        scratch_shapes=[pltpu.VMEM((s, TQ), F32), pltpu.VMEM((s, TQ), F32),
                        pltpu.VMEM((SUBLANES, TQ), F32), pltpu.VMEM((SUBLANES, TQ), F32),
                        pltpu.VMEM((V_EXT, TQ), F32)],
        compiler_params=pltpu.CompilerParams(
            dimension_semantics=("arbitrary",), vmem_limit_bytes=VMEM_LIMIT),
        name="attention",
    )(q_t.reshape(b * nh, HEAD_PAD, s), k.reshape(b * nh, s, HEAD_PAD),
      v_t.reshape(b * nh, V_EXT, s))


def _pad_cols(w, width):
    return jnp.pad(w, ((0, 0), (0, width - w.shape[1])))


def _prep_w_in(w_in):
    cuts = np.cumsum([SSD_INNER, CONV_CH, 2 * SSD_HEADS, Q_LORA, KV_LORA])
    w_z, w_xbc, w_dt, w_q, w_kv, w_kr = jnp.split(w_in, [int(c) for c in cuts], axis=1)
    t1, t2 = w_kr[:, :QK_ROPE // 2], w_kr[:, QK_ROPE // 2:]
    kra = _pad_cols(jnp.concatenate([t1, t2], axis=1), LANES)
    krb = _pad_cols(jnp.concatenate([-t2, t1], axis=1), LANES)
    return jnp.concatenate([w_z, w_xbc, _pad_cols(w_dt, LANES), w_q, w_kv, kra, krb],
                           axis=1).astype(BF16)


def _prep_w_q(w_q_b):
    return w_q_b.T.astype(BF16)


def _prep_w_kv(w_kv_b):
    w = w_kv_b.reshape(KV_LORA, MLA_HEADS, QK_NOPE + V_DIM)
    w_kn = w[:, :, :QK_NOPE].reshape(KV_LORA, MLA_HEADS * QK_NOPE).astype(BF16)
    w_v_t = w[:, :, QK_NOPE:].reshape(KV_LORA, MLA_HEADS * V_DIM).T.astype(BF16)
    return w_kn, w_v_t


def kernel(x, c, positions, ada_w, ada_b, norm_g, ffn_w_gate, ffn_w_up, ffn_w_down, w_in, conv_w,
           conv_b, dt_bias, a_log, d_skip, ssd_norm_g, q_norm_g, w_q_b, kv_norm_g, w_kv_b,
           attn_norm_g, w_out, final_norm_g):
    b, s, d = x.shape
    depth = ada_w.shape[0]
    assert s % TM_FFN == 0 and TM_FFN % ROWS_OUT == 0 and s % TM_MIX == 0
    assert s % (SSD_CHUNK * SSD_CHUNKS_PER_STEP) == 0
    assert s % TQ == 0 and s % TK_ATTN == 0 and s % ROPE_TS == 0

    c_pad = jnp.pad(c, ((0, SUBLANES - b), (0, 0)))
    mod_all = _ada(c_pad, ada_w, ada_b)
    tabs = _rope_tables(positions)
    fg = final_norm_g.reshape(1, d)

    for l in range(depth):
        mod = mod_all[l, :b].reshape(b, N_MOD, d)
        bf = lambda w: w.astype(BF16)
        x = _ffn(x, mod, norm_g[l, 0].reshape(1, d), bf(ffn_w_gate[l, 0]), bf(ffn_w_up[l, 0]),
                 bf(ffn_w_down[l, 0]), fg, row0=0, final=False)

        w_kn, w_v_t = _prep_w_kv(w_kv_b[l])
        dtb = _pad_cols(dt_bias[l].reshape(1, 2 * SSD_HEADS), LANES)
        z, xbc, dt, q_t, k, v_t = _mix_in(
            x, mod, norm_g[l, 1].reshape(1, d), _prep_w_in(w_in[l]), conv_w[l],
            conv_b[l].reshape(1, CONV_CH), dtb, q_norm_g[l].reshape(1, Q_LORA), _prep_w_q(w_q_b[l]),
            kv_norm_g[l].reshape(1, KV_LORA), w_kn, w_v_t, tabs)

        alog_pad = _pad_cols(a_log[l].reshape(1, 2 * SSD_HEADS), LANES)
        d_exp = jnp.repeat(d_skip[l], SSD_HEADDIM).reshape(1, SSD_INNER)
        y_f, y_b = _ssd(xbc, dt, alog_pad, d_exp)
        attn = _attention(q_t, k, v_t)

        mixer = (y_f, y_b, z, attn, ssd_norm_g[l].reshape(1, SSD_INNER),
                 attn_norm_g[l].reshape(1, MLA_HEADS * V_DIM),
                 bf(w_out[l, :SSD_INNER]), bf(w_out[l, SSD_INNER:]))
        x = _ffn(x, mod, norm_g[l, 2].reshape(1, d), bf(ffn_w_gate[l, 1]), bf(ffn_w_up[l, 1]),
                 bf(ffn_w_down[l, 1]), fg, row0=6, final=(l == depth - 1), mixer=mixer)
    return x
```

```python
import functools
import math

import jax
import jax.numpy as jnp
import numpy as np
from jax import lax
from jax.experimental import pallas as pl
from jax.experimental.pallas import tpu as pltpu

F32 = jnp.float32
BF16 = jnp.bfloat16

D_MODEL = 1024
D_FF = 2816
N_MOD = 9
SSD_INNER = 1024
SSD_HEADDIM = 64
SSD_HEADS = 16
SSD_GROUPS = 2
SSD_STATE = 128
SSD_CONV = 5
SSD_CHUNK = 128
SSD_CHUNKS_PER_STEP = 8
CONV_CH = SSD_INNER + 2 * SSD_GROUPS * SSD_STATE
MLA_HEADS = 8
QK_NOPE = 128
QK_ROPE = 64
V_DIM = 128
V_EXT = V_DIM + 16
Q_LORA = 512
KV_LORA = 256
ROPE_THETA = 10000.0
EPS = 1e-6

LANES = 128
SUBLANES = 8
HEAD_PAD = 256
VMEM_LIMIT = 56 * 1024 * 1024

C_Z = 0
C_XBC = C_Z + SSD_INNER
C_DT = C_XBC + CONV_CH
C_Q = C_DT + LANES
C_KV = C_Q + Q_LORA
C_KRA = C_KV + KV_LORA
C_KRB = C_KRA + LANES
IN_COLS_EXT = C_KRB + LANES

TM_FFN = 512
TM_MIX = 256
ROWS_OUT = 256
FF_SPLITS = ((0, 1536), (1536, D_FF))
TQ = 512
TK_ATTN = 8192
TK_SUB = 256
ROPE_TS = 1024
CONV_HALO = SUBLANES
Q_SCALE = (QK_NOPE + QK_ROPE) ** -0.5 * math.log2(math.e)


def _dot(a, b):
    return jnp.dot(a, b, preferred_element_type=F32)


def _dot_nt(a, b):
    return lax.dot_general(a, b, (((1,), (1,)), ((), ())), preferred_element_type=F32)


def _rms(x, g):
    return x * lax.rsqrt(jnp.mean(x * x, axis=-1, keepdims=True) + EPS) * g


def _modulate(x, g, shift, scale):
    return _rms(x, g) * (1.0 + scale) + shift


def _sigmoid(x):
    return 1.0 / (1.0 + jnp.exp(-x))


def _const_spec(shape):
    nd = len(shape)
    return pl.BlockSpec(shape, lambda *_: (0,) * nd, pipeline_mode=pl.Buffered(1))


def _ada_kernel(c_ref, w_ref, b_ref, o_ref):
    c = c_ref[...]
    c_act = (c * _sigmoid(c)).astype(BF16)
    o_ref[0] = _dot(c_act, w_ref[0].astype(BF16)) + b_ref[0]


def _ada(c_pad, ada_w, ada_b):
    depth, d, n = ada_w.shape
    tn = 1536
    return pl.pallas_call(
        _ada_kernel,
        out_shape=jax.ShapeDtypeStruct((depth, SUBLANES, n), F32),
        grid=(depth, n // tn),
        in_specs=[
            pl.BlockSpec((SUBLANES, d), lambda l, j: (0, 0)),
            pl.BlockSpec((1, d, tn), lambda l, j: (l, 0, j)),
            pl.BlockSpec((1, 1, tn), lambda l, j: (l, 0, j)),
        ],
        out_specs=pl.BlockSpec((1, SUBLANES, tn), lambda l, j: (l, 0, j)),
        compiler_params=pltpu.CompilerParams(
            dimension_semantics=("arbitrary", "arbitrary"), vmem_limit_bytes=VMEM_LIMIT),
        name="ada",
    )(c_pad, ada_w, ada_b.reshape(depth, 1, n))


def _rope_kernel(pos_ref, inv_ref, cos_t_ref, sin_t_ref, cos_n_ref, sin_n_ref):
    ang = inv_ref[...] * pos_ref[0].astype(F32)
    c, s = jnp.cos(ang), jnp.sin(ang)
    zero = jnp.zeros((LANES - QK_ROPE, ang.shape[1]), F32)
    c_t = jnp.concatenate([c, c, zero], axis=0)
    s_t = jnp.concatenate([s, s, zero], axis=0)
    cos_t_ref[0] = c_t
    sin_t_ref[0] = s_t
    cos_n_ref[0] = c_t.T
    sin_n_ref[0] = s_t.T


def _rope_tables(positions):
    b, s = positions.shape
    inv_freq = 1.0 / (ROPE_THETA ** (jnp.arange(0, QK_ROPE, 2, dtype=F32) / QK_ROPE))
    t_shape = jax.ShapeDtypeStruct((b, LANES, s), F32)
    n_shape = jax.ShapeDtypeStruct((b, s, LANES), F32)
    t_spec = pl.BlockSpec((1, LANES, ROPE_TS), lambda i, j: (i, 0, j))
    n_spec = pl.BlockSpec((1, ROPE_TS, LANES), lambda i, j: (i, j, 0))
    return pl.pallas_call(
        _rope_kernel,
        out_shape=(t_shape, t_shape, n_shape, n_shape),
        grid=(b, s // ROPE_TS),
        in_specs=[
            pl.BlockSpec((1, 1, ROPE_TS), lambda i, j: (i, 0, j)),
            pl.BlockSpec((QK_ROPE // 2, 1), lambda i, j: (0, 0)),
        ],
        out_specs=(t_spec, t_spec, n_spec, n_spec),
        compiler_params=pltpu.CompilerParams(dimension_semantics=("arbitrary", "arbitrary")),
        name="rope_tables",
    )(positions.reshape(b, 1, s), inv_freq.reshape(QK_ROPE // 2, 1))


def _ffn_tile(x, mod_ref, g_ref, wg_ref, wu_ref, wd_ref, fg_ref, o_ref, rows, row0, final):
    shift = mod_ref[0, row0:row0 + 1, :]
    scale = mod_ref[0, row0 + 1:row0 + 2, :]
    gate = mod_ref[0, row0 + 2:row0 + 3, :]
    h = _modulate(x, g_ref[...], shift, scale).astype(BF16)
    down = None
    for c0, c1 in FF_SPLITS:
        a = _dot(h, wg_ref[:, c0:c1])
        u = _dot(h, wu_ref[:, c0:c1])
        part = _dot((a * _sigmoid(a) * u).astype(BF16), wd_ref[c0:c1, :])
        down = part if down is None else down + part
    out = x + 0.5 * gate * down
    if final:
        out = _rms(out, fg_ref[...])
    o_ref[0, rows, :] = out


def _ffn_kernel(x_ref, mod_ref, g_ref, wg_ref, wu_ref, wd_ref, fg_ref, o_ref, *, row0, final):
    _ffn_tile(x_ref[0], mod_ref, g_ref, wg_ref, wu_ref, wd_ref, fg_ref, o_ref, slice(None),
              row0, final)


def _mix_out_ffn_kernel(x_ref, mod_ref, yf_ref, yb_ref, z_ref, at_ref, sg_ref, ag_ref,
                        wo_s_ref, wo_a_ref, g_ref, wg_ref, wu_ref, wd_ref, fg_ref,
                        o_ref, *, row0, final):
    gate = mod_ref[0, 5:6, :]
    gw = SSD_INNER // SSD_GROUPS
    for r0 in range(0, x_ref.shape[1], ROWS_OUT):
        rows = slice(r0, r0 + ROWS_OUT)
        attn = _rms(at_ref[0, rows, :].astype(F32), ag_ref[...]).astype(BF16)
        mixed = _dot(attn, wo_a_ref[...])
        for i in range(SSD_GROUPS):
            cols = slice(i * gw, (i + 1) * gw)
            z = z_ref[0, rows, cols].astype(F32)
            y = (yf_ref[0, rows, cols].astype(F32) + yb_ref[0, rows, cols].astype(F32)) \
                * (z * _sigmoid(z))
            mixed = mixed + _dot(_rms(y, sg_ref[:, cols]).astype(BF16), wo_s_ref[cols, :])
        _ffn_tile(x_ref[0, rows, :] + gate * mixed, mod_ref, g_ref, wg_ref, wu_ref, wd_ref,
                  fg_ref, o_ref, rows, row0, final)


def _ffn(x, mod, g, wg, wu, wd, fg, *, row0, final, mixer=None):
    b, s, d = x.shape
    tile = pl.BlockSpec((1, TM_FFN, d), lambda i, j: (i, j, 0))
    ffn_args = (g, wg, wu, wd, fg)
    if mixer is None:
        body, name, mix_args = _ffn_kernel, "ffn", ()
    else:
        body, name, mix_args = _mix_out_ffn_kernel, "mix_out_ffn", tuple(mixer)
    mix_specs = [tile] * 4 + [_const_spec(a.shape) for a in mix_args[4:]] if mix_args else []
    return pl.pallas_call(
        functools.partial(body, row0=row0, final=final),
        out_shape=jax.ShapeDtypeStruct(x.shape, F32),
        grid=(b, s // TM_FFN),
        in_specs=[tile, pl.BlockSpec((1, N_MOD, d), lambda i, j: (i, 0, 0))] + mix_specs
        + [_const_spec(a.shape) for a in ffn_args],
        out_specs=tile,
        compiler_params=pltpu.CompilerParams(
            dimension_semantics=("arbitrary", "arbitrary"), vmem_limit_bytes=VMEM_LIMIT),
        name=name,
    )(x, mod, *mix_args, *ffn_args)


def _mix_in_kernel(x_ref, xp_ref, xn_ref, mod_ref, g_ref, win_ref, cw_ref, cb_ref, dtb_ref,
                   qg_ref, wq_ref, kvg_ref, wkn_ref, wv_ref,
                   cos_t_ref, sin_t_ref, cos_n_ref, sin_n_ref,
                   z_ref, xbc_ref, dt_ref, q_ref, k_ref, v_ref, u_scr):
    j = pl.program_id(1)
    nj = pl.num_programs(1)
    tm = x_ref.shape[1]
    g = g_ref[...]
    shift = mod_ref[0, 3:4, :]
    scale = mod_ref[0, 4:5, :]

    h = _modulate(x_ref[0], g, shift, scale).astype(BF16)
    lat = _dot(h, win_ref[:, C_DT:])
    lat_off = lambda c: c - C_DT

    x_halo = jnp.concatenate([xp_ref[0], xn_ref[0]], axis=0)
    h_halo = _modulate(x_halo, g, shift, scale).astype(BF16)
    u_halo = _dot(h_halo, win_ref[:, C_XBC:C_DT])
    u_scr[0:CONV_HALO, :] = jnp.where(j > 0, u_halo[:CONV_HALO], 0.0)
    u_scr[CONV_HALO:CONV_HALO + tm, :] = _dot(h, win_ref[:, C_XBC:C_DT])
    u_scr[CONV_HALO + tm:, :] = jnp.where(j < nj - 1, u_halo[CONV_HALO:], 0.0)

    def conv_block(c):
        cols = slice(c * LANES, (c + 1) * LANES)
        u_ext = u_scr[:, cols]
        n_ext = u_ext.shape[0]
        acc = jnp.broadcast_to(cb_ref[:, cols], (tm, LANES))
        for k in range(SSD_CONV):
            shift = (SSD_CONV // 2 - k) % n_ext
            win = u_ext if shift == 0 else pltpu.roll(u_ext, shift, axis=0)
            acc = acc + cw_ref[k:k + 1, cols] * win[CONV_HALO:CONV_HALO + tm]
        xbc_ref[0, :, cols] = acc * _sigmoid(acc)

    n_conv = CONV_CH // LANES
    dt_ref[0] = jax.nn.softplus(lat[:, lat_off(C_DT):lat_off(C_Q)] + dtb_ref[...])
    hq = _rms(lat[:, lat_off(C_Q):lat_off(C_KV)], qg_ref[...]).astype(BF16)
    ckv = _rms(lat[:, lat_off(C_KV):lat_off(C_KRA)], kvg_ref[...]).astype(BF16)
    k_rope = (lat[:, lat_off(C_KRA):lat_off(C_KRB)] * cos_n_ref[0]
              + lat[:, lat_off(C_KRB):] * sin_n_ref[0]).astype(BF16)

    half = QK_ROPE // 2
    qk_dim = QK_NOPE + QK_ROPE
    cos_t = cos_t_ref[0, 0:half, :] * Q_SCALE
    sin_t = sin_t_ref[0, 0:half, :] * Q_SCALE
    for hd in range(MLA_HEADS):
        q_h = _dot_nt(wq_ref[hd * qk_dim:(hd + 1) * qk_dim, :], hq)
        v_h = _dot_nt(wv_ref[hd * V_DIM:(hd + 1) * V_DIM, :], ckv)
        if hd % 2 == 0:
            k_pair = _dot(ckv, wkn_ref[:, hd * QK_NOPE:(hd + 2) * QK_NOPE])
        conv_block(hd)
        q_ref[0, hd, 0:LANES, :] = (q_h[0:QK_NOPE] * Q_SCALE).astype(BF16)
        t1 = q_h[QK_NOPE:QK_NOPE + half]
        t2 = q_h[QK_NOPE + half:]
        q_ref[0, hd, LANES:LANES + half, :] = (t1 * cos_t - t2 * sin_t).astype(BF16)
        q_ref[0, hd, LANES + half:LANES + QK_ROPE, :] = (t2 * cos_t + t1 * sin_t).astype(BF16)
        q_ref[0, hd, LANES + QK_ROPE:, :] = jnp.zeros((HEAD_PAD - LANES - QK_ROPE, tm), BF16)
        k_ref[0, hd, :, 0:LANES] = k_pair[:, (hd % 2) * QK_NOPE:(hd % 2 + 1) * QK_NOPE].astype(BF16)
        k_ref[0, hd, :, LANES:] = k_rope
        v_ref[0, hd, 0:V_DIM, :] = v_h.astype(BF16)
        v_ref[0, hd, V_DIM:, :] = jnp.ones((V_EXT - V_DIM, tm), BF16)

    z_cols = 2 * LANES
    for c in range(SSD_INNER // z_cols):
        z_ref[0, :, c * z_cols:(c + 1) * z_cols] = _dot(
            h, win_ref[:, C_Z + c * z_cols:C_Z + (c + 1) * z_cols]).astype(BF16)
        conv_block(MLA_HEADS + c)


def _mix_in(x, mod, g, win, cw, cb, dtb, qg, wq, kvg, wkn, wv, tabs):
    b, s, d = x.shape
    tm = TM_MIX
    cos_t, sin_t, cos_n, sin_n = tabs
    nblk = s // CONV_HALO
    per = tm // CONV_HALO
    out_shape = (
        jax.ShapeDtypeStruct((b, s, SSD_INNER), BF16),
        jax.ShapeDtypeStruct((b, s, CONV_CH), F32),
        jax.ShapeDtypeStruct((b, s, LANES), F32),
        jax.ShapeDtypeStruct((b, MLA_HEADS, HEAD_PAD, s), BF16),
        jax.ShapeDtypeStruct((b, MLA_HEADS, s, HEAD_PAD), BF16),
        jax.ShapeDtypeStruct((b, MLA_HEADS, V_EXT, s), BF16),
    )
    tok = lambda w: pl.BlockSpec((1, tm, w), lambda i, j: (i, j, 0))
    in_specs = [
        tok(d),
        pl.BlockSpec((1, CONV_HALO, d), lambda i, j: (i, jnp.maximum(j * per - 1, 0), 0)),
        pl.BlockSpec((1, CONV_HALO, d), lambda i, j: (i, jnp.minimum((j + 1) * per, nblk - 1), 0)),
        pl.BlockSpec((1, N_MOD, d), lambda i, j: (i, 0, 0)),
        _const_spec((1, d)),
        _const_spec(win.shape),
        _const_spec(cw.shape),
        _const_spec(cb.shape),
        _const_spec(dtb.shape),
        _const_spec(qg.shape),
        _const_spec(wq.shape),
        _const_spec(kvg.shape),
        _const_spec(wkn.shape),
        _const_spec(wv.shape),
        pl.BlockSpec((1, LANES, tm), lambda i, j: (i, 0, j)),
        pl.BlockSpec((1, LANES, tm), lambda i, j: (i, 0, j)),
        tok(LANES),
        tok(LANES),
    ]
    out_specs = (
        tok(SSD_INNER),
        tok(CONV_CH),
        tok(LANES),
        pl.BlockSpec((1, MLA_HEADS, HEAD_PAD, tm), lambda i, j: (i, 0, 0, j)),
        pl.BlockSpec((1, MLA_HEADS, tm, HEAD_PAD), lambda i, j: (i, 0, j, 0)),
        pl.BlockSpec((1, MLA_HEADS, V_EXT, tm), lambda i, j: (i, 0, 0, j)),
    )
    return pl.pallas_call(
        _mix_in_kernel,
        out_shape=out_shape,
        grid=(b, s // tm),
        in_specs=in_specs,
        out_specs=out_specs,
        scratch_shapes=[pltpu.VMEM((tm + 2 * CONV_HALO, CONV_CH), F32)],
        compiler_params=pltpu.CompilerParams(
            dimension_semantics=("arbitrary", "arbitrary"), vmem_limit_bytes=VMEM_LIMIT),
        name="mix_in",
    )(x, x, x, mod, g, win, cw, cb, dtb, qg, wq, kvg, wkn, wv, cos_t, sin_t, cos_n, sin_n)


def _split3(x):
    hi = x.astype(BF16)
    r1 = x - hi.astype(F32)
    mid = r1.astype(BF16)
    lo = (r1 - mid.astype(F32)).astype(BF16)
    return hi, mid, lo


def _ssd_direction(xbc, dt, a_row, state_ref, y_ref, y_rows, d_row, *, backward):
    L = SSD_CHUNK
    pair_w = 2 * SSD_HEADDIM
    pairs_per_group = SSD_HEADS // SSD_GROUPS // 2
    off = SSD_HEADS if backward else 0
    edge = 0 if backward else L - 1
    row_i = lax.broadcasted_iota(jnp.int32, (L, L), 0)
    col_i = lax.broadcasted_iota(jnp.int32, (L, L), 1)
    keep = (row_i <= col_i) if backward else (row_i >= col_i)
    left = col_i < SSD_HEADDIM
    tri = jnp.where(keep, 1.0, 0.0).astype(BF16)
    a_dt = dt * a_row
    hi, mid, lo = _split3(a_dt)
    a_cs = _dot(tri, hi) + _dot(tri, mid) + _dot(tri, lo)
    a_cs_t = a_cs.T
    src_t = (a_cs - jnp.log2(dt)).T
    yield
    c_off = SSD_INNER + SSD_GROUPS * SSD_STATE
    for grp in range(SSD_GROUPS):
        b_g = xbc[:, SSD_INNER + grp * SSD_STATE:SSD_INNER + (grp + 1) * SSD_STATE]
        c_bf = xbc[:, c_off + grp * SSD_STATE:c_off + (grp + 1) * SSD_STATE].astype(BF16)
        cb = _dot_nt(c_bf, b_g.astype(BF16))
        b_t = b_g.T
        states = [state_ref[grp * pairs_per_group + j] for j in range(pairs_per_group)]
        y_off = _dot(c_bf, jnp.concatenate(states, axis=1).astype(BF16))
        yield
        for j in range(pairs_per_group):
            pr = grp * pairs_per_group + j
            xs_pair = xbc[:, pr * pair_w:(pr + 1) * pair_w]
            rhs = jnp.concatenate([jnp.where(left, xs_pair, 0.0).astype(BF16),
                                   jnp.where(left, 0.0, xs_pair).astype(BF16)], axis=0)
            m_parts, w_parts, col_parts, tot_parts = [], [], [], []
            for hd in (2 * pr, 2 * pr + 1):
                col = jnp.broadcast_to(a_cs[:, off + hd:off + hd + 1], (L, L))
                src = src_t[off + hd:off + hd + 1, :]
                decay_dt = jnp.exp2(jnp.where(keep, col - src, -jnp.inf))
                m_parts.append((cb * decay_dt).astype(BF16))
                a_tot = a_cs_t[off + hd:off + hd + 1, edge:edge + 1]
                w_parts.append((b_t * jnp.exp2(a_tot - src)).astype(BF16))
                col_parts.append(col)
                tot_parts.append(a_tot)
            y = _dot(jnp.concatenate(m_parts, axis=1), rhs)
            upd = _dot(jnp.concatenate(w_parts, axis=1), rhs)
            yield
            y = y + jnp.exp2(jnp.where(left, col_parts[0], col_parts[1])) \
                * y_off[:, j * pair_w:(j + 1) * pair_w]
            if d_row is not None:
                y = y + d_row[:, pr * pair_w:(pr + 1) * pair_w] * xs_pair
            y_ref[0, y_rows, pr * pair_w:(pr + 1) * pair_w] = y.astype(BF16)
            keep_frac = jnp.exp2(jnp.where(left[0:1], tot_parts[0], tot_parts[1]))
            state_ref[pr] = keep_frac * states[j] + upd
            yield


def _ssd_kernel(xf_ref, xb_ref, dtf_ref, dtb_ref, alog_ref, d_ref, yf_ref, yb_ref, sf_ref, sb_ref):
    @pl.when(pl.program_id(1) == 0)
    def _():
        sf_ref[...] = jnp.zeros_like(sf_ref)
        sb_ref[...] = jnp.zeros_like(sb_ref)

    a_row = -jnp.exp(alog_ref[...]) * math.log2(math.e)
    for k in range(SSD_CHUNKS_PER_STEP):
        rf = slice(k * SSD_CHUNK, (k + 1) * SSD_CHUNK)
        rb = slice((SSD_CHUNKS_PER_STEP - 1 - k) * SSD_CHUNK, (SSD_CHUNKS_PER_STEP - k) * SSD_CHUNK)
        fwd = _ssd_direction(xf_ref[0, rf, :], dtf_ref[0, rf, :], a_row, sf_ref, yf_ref, rf,
                             d_ref[...], backward=False)
        bwd = _ssd_direction(xb_ref[0, rb, :], dtb_ref[0, rb, :], a_row, sb_ref, yb_ref, rb,
                             None, backward=True)
        for _ in zip(fwd, bwd):
            pass


def _ssd(xbc, dt, alog_pad, d_exp):
    b, s, _ = xbc.shape
    rows = SSD_CHUNKS_PER_STEP * SSD_CHUNK
    nc = s // rows
    fwd = lambda w: pl.BlockSpec((1, rows, w), lambda i, c: (i, c, 0))
    bwd = lambda w: pl.BlockSpec((1, rows, w), lambda i, c: (i, nc - 1 - c, 0))
    y_shape = jax.ShapeDtypeStruct((b, s, SSD_INNER), BF16)
    state = pltpu.VMEM((SSD_HEADS // 2, SSD_STATE, 2 * SSD_HEADDIM), F32)
    return pl.pallas_call(
        _ssd_kernel,
        out_shape=(y_shape, y_shape),
        grid=(b, nc),
        in_specs=[fwd(CONV_CH), bwd(CONV_CH), fwd(LANES), bwd(LANES),
                  _const_spec((1, LANES)), _const_spec((1, SSD_INNER))],
        out_specs=(fwd(SSD_INNER), bwd(SSD_INNER)),
        scratch_shapes=[state, state],
        compiler_params=pltpu.CompilerParams(
            dimension_semantics=("arbitrary", "arbitrary"), vmem_limit_bytes=VMEM_LIMIT),
        name="ssd",
    )(xbc, xbc, dt, dt, alog_pad, d_exp)


def _fold_rows(t, op):
    r, c = t.shape
    return op(t.reshape(r // SUBLANES, SUBLANES, c), axis=0)


def _attn_kernel(q_ref, k_ref, v_ref, o_ref, sa_scr, sb_scr, ma_scr, mb_scr, acc_scr):
    t = pl.program_id(0)
    q_t = q_ref[0]
    tq = q_t.shape[1]
    n_steps = k_ref.shape[1] // TK_ATTN
    n_sub = TK_ATTN // TK_SUB
    tq_part = tq // n_steps

    def scores(s_new, j, m_run, tk):
        start = pl.multiple_of(j * tk, tk)
        s_t = _dot(k_ref[0, pl.ds(start, tk), :], q_t)
        s_new[pl.ds(start, tk), :] = s_t
        return jnp.maximum(m_run, _fold_rows(s_t, jnp.max))

    def values(s_old, j, m, acc, tk):
        start = pl.multiple_of(j * tk, tk)
        p_t = jnp.exp2(s_old[pl.ds(start, tk), :] - m)
        return acc + _dot(v_ref[0, :, pl.ds(start, tk)], p_t.astype(BF16))

    def finish(i):
        cols = pl.ds(pl.multiple_of(i * tq_part, tq_part), tq_part)
        out = (acc_scr[0:V_DIM, cols] / acc_scr[V_DIM:V_DIM + 1, cols]).T
        o_ref[0, cols, :] = out.astype(BF16)

    m_init = jnp.full((SUBLANES, tq), -jnp.inf, F32)
    acc_init = jnp.zeros((V_EXT, tq), F32)

    @pl.when(t == 0)
    def _():
        ma_scr[...] = lax.fori_loop(
            0, n_steps, lambda i, m_run: scores(sa_scr, i, m_run, TK_ATTN), m_init)
        acc_scr[...] = jnp.ones_like(acc_scr)

    def steady(s_new, m_new, s_old, m_old):
        m = jnp.max(m_old[...], axis=0, keepdims=True)

        def body(i, carry):
            m_run, acc = carry
            for u in range(n_sub):
                acc = values(s_old, i * n_sub + u, m, acc, TK_SUB)
                m_run = scores(s_new, i * n_sub + u, m_run, TK_SUB)
                if u == n_sub // 2:
                    finish(i)
            return m_run, acc

        m_run, acc = lax.fori_loop(0, n_steps, body, (m_init, acc_init))
        m_new[...] = m_run
        acc_scr[...] = acc

    odd = lax.rem(t, 2) == 1
    pl.when(odd)(lambda: steady(sb_scr, mb_scr, sa_scr, ma_scr))
    pl.when(jnp.logical_and(t > 0, jnp.logical_not(odd)))(
        lambda: steady(sa_scr, ma_scr, sb_scr, mb_scr))


def _attention(q_t, k, v_t):
    b, nh, _, s = q_t.shape
    n_q = s // TQ
    n_items = b * nh * n_q
    item = lambda t, lag: jnp.clip(t - lag, 0, n_items - 1)
    return pl.pallas_call(
        _attn_kernel,
        out_shape=jax.ShapeDtypeStruct((b, s, nh * V_DIM), BF16),
        grid=(n_items + 2,),
        in_specs=[
            pl.BlockSpec((1, HEAD_PAD, TQ), lambda t: (item(t, 0) // n_q, 0, item(t, 0) % n_q)),
            pl.BlockSpec((1, s, HEAD_PAD), lambda t: (item(t, 0) // n_q, 0, 0)),
            pl.BlockSpec((1, V_EXT, s), lambda t: (item(t, 1) // n_q, 0, 0)),
        ],
        out_specs=pl.BlockSpec(
            (1, TQ, V_DIM),
            lambda t: (item(t, 2) // (nh * n_q), item(t, 2) % n_q, (item(t, 2) // n_q) % nh)),
        scratch_shapes=[pltpu.VMEM((s, TQ), F32), pltpu.VMEM((s, TQ), F32),
                        pltpu.VMEM((SUBLANES, TQ), F32), pltpu.VMEM((SUBLANES, TQ), F32),
                        pltpu.VMEM((V_EXT, TQ), F32)],
        compiler_params=pltpu.CompilerParams(
            dimension_semantics=("arbitrary",), vmem_limit_bytes=VMEM_LIMIT),
        name="attention",
    )(q_t.reshape(b * nh, HEAD_PAD, s), k.reshape(b * nh, s, HEAD_PAD),
      v_t.reshape(b * nh, V_EXT, s))


def _pad_cols(w, width):
    return jnp.pad(w, ((0, 0), (0, width - w.shape[1])))


def _prep_w_in(w_in):
    cuts = np.cumsum([SSD_INNER, CONV_CH, 2 * SSD_HEADS, Q_LORA, KV_LORA])
    w_z, w_xbc, w_dt, w_q, w_kv, w_kr = jnp.split(w_in, [int(c) for c in cuts], axis=1)
    t1, t2 = w_kr[:, :QK_ROPE // 2], w_kr[:, QK_ROPE // 2:]
    kra = _pad_cols(jnp.concatenate([t1, t2], axis=1), LANES)
    krb = _pad_cols(jnp.concatenate([-t2, t1], axis=1), LANES)
    return jnp.concatenate([w_z, w_xbc, _pad_cols(w_dt, LANES), w_q, w_kv, kra, krb],
                           axis=1).astype(BF16)


def _prep_w_q(w_q_b):
    return w_q_b.T.astype(BF16)


def _prep_w_kv(w_kv_b):
    w = w_kv_b.reshape(KV_LORA, MLA_HEADS, QK_NOPE + V_DIM)
    w_kn = w[:, :, :QK_NOPE].reshape(KV_LORA, MLA_HEADS * QK_NOPE).astype(BF16)
    w_v_t = w[:, :, QK_NOPE:].reshape(KV_LORA, MLA_HEADS * V_DIM).T.astype(BF16)
    return w_kn, w_v_t


def kernel(x, c, positions, ada_w, ada_b, norm_g, ffn_w_gate, ffn_w_up, ffn_w_down, w_in, conv_w,
           conv_b, dt_bias, a_log, d_skip, ssd_norm_g, q_norm_g, w_q_b, kv_norm_g, w_kv_b,
           attn_norm_g, w_out, final_norm_g):
    b, s, d = x.shape
    depth = ada_w.shape[0]
    assert s % TM_FFN == 0 and TM_FFN % ROWS_OUT == 0 and s % TM_MIX == 0
    assert s % (SSD_CHUNK * SSD_CHUNKS_PER_STEP) == 0
    assert s % TQ == 0 and s % TK_ATTN == 0 and s % ROPE_TS == 0

    c_pad = jnp.pad(c, ((0, SUBLANES - b), (0, 0)))
    mod_all = _ada(c_pad, ada_w, ada_b)
    tabs = _rope_tables(positions)
    fg = final_norm_g.reshape(1, d)

    for l in range(depth):
        mod = mod_all[l, :b].reshape(b, N_MOD, d)
        bf = lambda w: w.astype(BF16)
        x = _ffn(x, mod, norm_g[l, 0].reshape(1, d), bf(ffn_w_gate[l, 0]), bf(ffn_w_up[l, 0]),
                 bf(ffn_w_down[l, 0]), fg, row0=0, final=False)

        w_kn, w_v_t = _prep_w_kv(w_kv_b[l])
        dtb = _pad_cols(dt_bias[l].reshape(1, 2 * SSD_HEADS), LANES)
        z, xbc, dt, q_t, k, v_t = _mix_in(
            x, mod, norm_g[l, 1].reshape(1, d), _prep_w_in(w_in[l]), conv_w[l],
            conv_b[l].reshape(1, CONV_CH), dtb, q_norm_g[l].reshape(1, Q_LORA), _prep_w_q(w_q_b[l]),
            kv_norm_g[l].reshape(1, KV_LORA), w_kn, w_v_t, tabs)

        alog_pad = _pad_cols(a_log[l].reshape(1, 2 * SSD_HEADS), LANES)
        d_exp = jnp.repeat(d_skip[l], SSD_HEADDIM).reshape(1, SSD_INNER)
        y_f, y_b = _ssd(xbc, dt, alog_pad, d_exp)
        attn = _attention(q_t, k, v_t)

        mixer = (y_f, y_b, z, attn, ssd_norm_g[l].reshape(1, SSD_INNER),
                 attn_norm_g[l].reshape(1, MLA_HEADS * V_DIM),
                 bf(w_out[l, :SSD_INNER]), bf(w_out[l, SSD_INNER:]))
        x = _ffn(x, mod, norm_g[l, 2].reshape(1, d), bf(ffn_w_gate[l, 1]), bf(ffn_w_up[l, 1]),
                 bf(ffn_w_down[l, 1]), fg, row0=6, final=(l == depth - 1), mixer=mixer)
    return x
```

```python
import functools
import math

import jax
import jax.numpy as jnp
import numpy as np
from jax import lax
from jax.experimental import pallas as pl
from jax.experimental.pallas import tpu as pltpu

F32 = jnp.float32
BF16 = jnp.bfloat16

D_MODEL = 1024
D_FF = 2816
N_MOD = 9
SSD_INNER = 1024
SSD_HEADDIM = 64
SSD_HEADS = 16
SSD_GROUPS = 2
SSD_STATE = 128
SSD_CONV = 5
SSD_CHUNK = 128
SSD_CHUNKS_PER_STEP = 8
CONV_CH = SSD_INNER + 2 * SSD_GROUPS * SSD_STATE
MLA_HEADS = 8
QK_NOPE = 128
QK_ROPE = 64
V_DIM = 128
V_EXT = V_DIM + 16
Q_LORA = 512
KV_LORA = 256
ROPE_THETA = 10000.0
EPS = 1e-6

LANES = 128
SUBLANES = 8
HEAD_PAD = 256
VMEM_LIMIT = 56 * 1024 * 1024

C_Z = 0
C_XBC = C_Z + SSD_INNER
C_DT = C_XBC + CONV_CH
C_Q = C_DT + LANES
C_KV = C_Q + Q_LORA
C_KRA = C_KV + KV_LORA
IN_COLS_EXT = C_KRA + LANES

TM_FFN = 512
TM_MIX = 256
ROWS_OUT = 256
FF_SPLITS = ((0, 1536), (1536, D_FF))
TQ = 512
TK_ATTN = 8192
TK_SUB = 256
ROPE_TS = 1024
CONV_HALO = SUBLANES
Q_SCALE = (QK_NOPE + QK_ROPE) ** -0.5 * math.log2(math.e)


def _dot(a, b):
    return jnp.dot(a, b, preferred_element_type=F32)


def _dot_nt(a, b):
    return lax.dot_general(a, b, (((1,), (1,)), ((), ())), preferred_element_type=F32)


def _rms(x, g):
    return x * lax.rsqrt(jnp.mean(x * x, axis=-1, keepdims=True) + EPS) * g


def _modulate(x, g, shift, scale):
    return _rms(x, g) * (1.0 + scale) + shift


def _sigmoid(x):
    return 1.0 / (1.0 + jnp.exp(-x))


def _const_spec(shape):
    nd = len(shape)
    return pl.BlockSpec(shape, lambda *_: (0,) * nd, pipeline_mode=pl.Buffered(1))


def _ada_kernel(c_ref, w_ref, b_ref, o_ref):
    c = c_ref[...]
    c_act = (c * _sigmoid(c)).astype(BF16)
    o_ref[0] = _dot(c_act, w_ref[0].astype(BF16)) + b_ref[0]


def _ada(c_pad, ada_w, ada_b):
    depth, d, n = ada_w.shape
    tn = 1536
    return pl.pallas_call(
        _ada_kernel,
        out_shape=jax.ShapeDtypeStruct((depth, SUBLANES, n), F32),
        grid=(depth, n // tn),
        in_specs=[
            pl.BlockSpec((SUBLANES, d), lambda l, j: (0, 0)),
            pl.BlockSpec((1, d, tn), lambda l, j: (l, 0, j)),
            pl.BlockSpec((1, 1, tn), lambda l, j: (l, 0, j)),
        ],
        out_specs=pl.BlockSpec((1, SUBLANES, tn), lambda l, j: (l, 0, j)),
        compiler_params=pltpu.CompilerParams(
            dimension_semantics=("arbitrary", "arbitrary"), vmem_limit_bytes=VMEM_LIMIT),
        name="ada",
    )(c_pad, ada_w, ada_b.reshape(depth, 1, n))


def _rope_kernel(pos_ref, inv_ref, cos_t_ref, sin_t_ref, cos_n_ref, sin_n_ref):
    ang = inv_ref[...] * pos_ref[0].astype(F32)
    c, s = jnp.cos(ang), jnp.sin(ang)
    zero = jnp.zeros((LANES - QK_ROPE, ang.shape[1]), F32)
    c_t = jnp.concatenate([c, c, zero], axis=0)
    s_t = jnp.concatenate([s, s, zero], axis=0)
    cos_t_ref[0] = c_t
    sin_t_ref[0] = s_t
    cos_n_ref[0] = c_t.T
    sin_n_ref[0] = s_t.T


def _rope_tables(positions):
    b, s = positions.shape
    inv_freq = 1.0 / (ROPE_THETA ** (jnp.arange(0, QK_ROPE, 2, dtype=F32) / QK_ROPE))
    t_shape = jax.ShapeDtypeStruct((b, LANES, s), F32)
    n_shape = jax.ShapeDtypeStruct((b, s, LANES), F32)
    t_spec = pl.BlockSpec((1, LANES, ROPE_TS), lambda i, j: (i, 0, j))
    n_spec = pl.BlockSpec((1, ROPE_TS, LANES), lambda i, j: (i, j, 0))
    return pl.pallas_call(
        _rope_kernel,
        out_shape=(t_shape, t_shape, n_shape, n_shape),
        grid=(b, s // ROPE_TS),
        in_specs=[
            pl.BlockSpec((1, 1, ROPE_TS), lambda i, j: (i, 0, j)),
            pl.BlockSpec((QK_ROPE // 2, 1), lambda i, j: (0, 0)),
        ],
        out_specs=(t_spec, t_spec, n_spec, n_spec),
        compiler_params=pltpu.CompilerParams(dimension_semantics=("arbitrary", "arbitrary")),
        name="rope_tables",
    )(positions.reshape(b, 1, s), inv_freq.reshape(QK_ROPE // 2, 1))


def _ffn_tile(x, mod_ref, g_ref, wg_ref, wu_ref, wd_ref, fg_ref, o_ref, rows, row0, final):
    shift = mod_ref[0, row0:row0 + 1, :]
    scale = mod_ref[0, row0 + 1:row0 + 2, :]
    gate = mod_ref[0, row0 + 2:row0 + 3, :]
    h = _modulate(x, g_ref[...], shift, scale).astype(BF16)
    down = None
    for c0, c1 in FF_SPLITS:
        a = _dot(h, wg_ref[:, c0:c1])
        u = _dot(h, wu_ref[:, c0:c1])
        part = _dot((a * _sigmoid(a) * u).astype(BF16), wd_ref[c0:c1, :])
        down = part if down is None else down + part
    out = x + 0.5 * gate * down
    if final:
        out = _rms(out, fg_ref[...])
    o_ref[0, rows, :] = out


def _ffn_kernel(x_ref, mod_ref, g_ref, wg_ref, wu_ref, wd_ref, fg_ref, o_ref, *, row0, final):
    _ffn_tile(x_ref[0], mod_ref, g_ref, wg_ref, wu_ref, wd_ref, fg_ref, o_ref, slice(None),
              row0, final)


def _mix_out_ffn_kernel(x_ref, mod_ref, yf_ref, yb_ref, z_ref, at_ref, sg_ref, ag_ref,
                        wo_s_ref, wo_a_ref, g_ref, wg_ref, wu_ref, wd_ref, fg_ref,
                        o_ref, *, row0, final):
    gate = mod_ref[0, 5:6, :]
    gw = SSD_INNER // SSD_GROUPS
    for r0 in range(0, x_ref.shape[1], ROWS_OUT):
        rows = slice(r0, r0 + ROWS_OUT)
        attn = _rms(at_ref[0, rows, :].astype(F32), ag_ref[...]).astype(BF16)
        mixed = _dot(attn, wo_a_ref[...])
        for i in range(SSD_GROUPS):
            cols = slice(i * gw, (i + 1) * gw)
            z = z_ref[0, rows, cols].astype(F32)
            y = (yf_ref[0, rows, cols].astype(F32) + yb_ref[0, rows, cols].astype(F32)) \
                * (z * _sigmoid(z))
            mixed = mixed + _dot(_rms(y, sg_ref[:, cols]).astype(BF16), wo_s_ref[cols, :])
        _ffn_tile(x_ref[0, rows, :] + gate * mixed, mod_ref, g_ref, wg_ref, wu_ref, wd_ref,
                  fg_ref, o_ref, rows, row0, final)


def _ffn(x, mod, g, wg, wu, wd, fg, *, row0, final, mixer=None):
    b, s, d = x.shape
    tile = pl.BlockSpec((1, TM_FFN, d), lambda i, j: (i, j, 0))
    ffn_args = (g, wg, wu, wd, fg)
    if mixer is None:
        body, name, mix_args = _ffn_kernel, "ffn", ()
    else:
        body, name, mix_args = _mix_out_ffn_kernel, "mix_out_ffn", tuple(mixer)
    mix_specs = [tile] * 4 + [_const_spec(a.shape) for a in mix_args[4:]] if mix_args else []
    return pl.pallas_call(
        functools.partial(body, row0=row0, final=final),
        out_shape=jax.ShapeDtypeStruct(x.shape, F32),
        grid=(b, s // TM_FFN),
        in_specs=[tile, pl.BlockSpec((1, N_MOD, d), lambda i, j: (i, 0, 0))] + mix_specs
        + [_const_spec(a.shape) for a in ffn_args],
        out_specs=tile,
        compiler_params=pltpu.CompilerParams(
            dimension_semantics=("arbitrary", "arbitrary"), vmem_limit_bytes=VMEM_LIMIT),
        name=name,
    )(x, mod, *mix_args, *ffn_args)


def _mix_in_kernel(x_ref, xp_ref, xn_ref, mod_ref, g_ref, win_ref, cw_ref, cb_ref, dtb_ref,
                   qg_ref, wq_ref, kvg_ref, wkn_ref, wv_ref,
                   cos_t_ref, sin_t_ref, cos_n_ref, sin_n_ref,
                   z_ref, xbc_ref, dt_ref, q_ref, k_ref, v_ref, u_scr):
    j = pl.program_id(1)
    nj = pl.num_programs(1)
    tm = x_ref.shape[1]
    g = g_ref[...]
    shift = mod_ref[0, 3:4, :]
    scale = mod_ref[0, 4:5, :]

    h = _modulate(x_ref[0], g, shift, scale).astype(BF16)
    lat = _dot(h, win_ref[:, C_DT:])
    lat_off = lambda c: c - C_DT

    x_halo = jnp.concatenate([xp_ref[0], xn_ref[0]], axis=0)
    h_halo = _modulate(x_halo, g, shift, scale).astype(BF16)
    u_halo = _dot(h_halo, win_ref[:, C_XBC:C_DT])
    u_scr[0:CONV_HALO, :] = jnp.where(j > 0, u_halo[:CONV_HALO], 0.0)
    u_scr[CONV_HALO:CONV_HALO + tm, :] = _dot(h, win_ref[:, C_XBC:C_DT])
    u_scr[CONV_HALO + tm:, :] = jnp.where(j < nj - 1, u_halo[CONV_HALO:], 0.0)

    def conv_block(c):
        cols = slice(c * LANES, (c + 1) * LANES)
        u_ext = u_scr[:, cols]
        n_ext = u_ext.shape[0]
        acc = jnp.broadcast_to(cb_ref[:, cols], (tm, LANES))
        for k in range(SSD_CONV):
            shift = (SSD_CONV // 2 - k) % n_ext
            win = u_ext if shift == 0 else pltpu.roll(u_ext, shift, axis=0)
            acc = acc + cw_ref[k:k + 1, cols] * win[CONV_HALO:CONV_HALO + tm]
        xbc_ref[0, :, cols] = acc * _sigmoid(acc)

    n_conv = CONV_CH // LANES
    dt_ref[0] = jax.nn.softplus(lat[:, lat_off(C_DT):lat_off(C_Q)] + dtb_ref[...])
    hq = _rms(lat[:, lat_off(C_Q):lat_off(C_KV)], qg_ref[...]).astype(BF16)
    ckv = _rms(lat[:, lat_off(C_KV):lat_off(C_KRA)], kvg_ref[...]).astype(BF16)
    kr = lat[:, lat_off(C_KRA):]
    lane = lax.broadcasted_iota(jnp.int32, kr.shape, 1)
    kr_rot = jnp.where(lane < QK_ROPE // 2, -pltpu.roll(kr, LANES - QK_ROPE // 2, axis=1),
                       pltpu.roll(kr, QK_ROPE // 2, axis=1))
    k_rope = (kr * cos_n_ref[0] + kr_rot * sin_n_ref[0]).astype(BF16)

    half = QK_ROPE // 2
    qk_dim = QK_NOPE + QK_ROPE
    cos_t = cos_t_ref[0, 0:half, :] * Q_SCALE
    sin_t = sin_t_ref[0, 0:half, :] * Q_SCALE
    for hd in range(MLA_HEADS):
        q_h = _dot_nt(wq_ref[hd * qk_dim:(hd + 1) * qk_dim, :], hq)
        v_h = _dot_nt(wv_ref[hd * V_DIM:(hd + 1) * V_DIM, :], ckv)
        if hd % 2 == 0:
            k_pair = _dot(ckv, wkn_ref[:, hd * QK_NOPE:(hd + 2) * QK_NOPE])
        conv_block(hd)
        q_ref[0, hd, 0:LANES, :] = (q_h[0:QK_NOPE] * Q_SCALE).astype(BF16)
        t1 = q_h[QK_NOPE:QK_NOPE + half]
        t2 = q_h[QK_NOPE + half:]
        q_ref[0, hd, LANES:LANES + half, :] = (t1 * cos_t - t2 * sin_t).astype(BF16)
        q_ref[0, hd, LANES + half:LANES + QK_ROPE, :] = (t2 * cos_t + t1 * sin_t).astype(BF16)
        q_ref[0, hd, LANES + QK_ROPE:, :] = jnp.zeros((HEAD_PAD - LANES - QK_ROPE, tm), BF16)
        k_ref[0, hd, :, 0:LANES] = k_pair[:, (hd % 2) * QK_NOPE:(hd % 2 + 1) * QK_NOPE].astype(BF16)
        k_ref[0, hd, :, LANES:] = k_rope
        v_ref[0, hd, 0:V_DIM, :] = v_h.astype(BF16)
        v_ref[0, hd, V_DIM:, :] = jnp.ones((V_EXT - V_DIM, tm), BF16)

    z_cols = 2 * LANES
    for c in range(SSD_INNER // z_cols):
        z_ref[0, :, c * z_cols:(c + 1) * z_cols] = _dot(
            h, win_ref[:, C_Z + c * z_cols:C_Z + (c + 1) * z_cols]).astype(BF16)
        conv_block(MLA_HEADS + c)


def _mix_in(x, mod, g, win, cw, cb, dtb, qg, wq, kvg, wkn, wv, tabs):
    b, s, d = x.shape
    tm = TM_MIX
    cos_t, sin_t, cos_n, sin_n = tabs
    nblk = s // CONV_HALO
    per = tm // CONV_HALO
    out_shape = (
        jax.ShapeDtypeStruct((b, s, SSD_INNER), BF16),
        jax.ShapeDtypeStruct((b, s, CONV_CH), F32),
        jax.ShapeDtypeStruct((b, s, LANES), F32),
        jax.ShapeDtypeStruct((b, MLA_HEADS, HEAD_PAD, s), BF16),
        jax.ShapeDtypeStruct((b, MLA_HEADS, s, HEAD_PAD), BF16),
        jax.ShapeDtypeStruct((b, MLA_HEADS, V_EXT, s), BF16),
    )
    tok = lambda w: pl.BlockSpec((1, tm, w), lambda i, j: (i, j, 0))
    in_specs = [
        tok(d),
        pl.BlockSpec((1, CONV_HALO, d), lambda i, j: (i, jnp.maximum(j * per - 1, 0), 0)),
        pl.BlockSpec((1, CONV_HALO, d), lambda i, j: (i, jnp.minimum((j + 1) * per, nblk - 1), 0)),
        pl.BlockSpec((1, N_MOD, d), lambda i, j: (i, 0, 0)),
        _const_spec((1, d)),
        _const_spec(win.shape),
        _const_spec(cw.shape),
        _const_spec(cb.shape),
        _const_spec(dtb.shape),
        _const_spec(qg.shape),
        _const_spec(wq.shape),
        _const_spec(kvg.shape),
        _const_spec(wkn.shape),
        _const_spec(wv.shape),
        pl.BlockSpec((1, LANES, tm), lambda i, j: (i, 0, j)),
        pl.BlockSpec((1, LANES, tm), lambda i, j: (i, 0, j)),
        tok(LANES),
        tok(LANES),
    ]
    out_specs = (
        tok(SSD_INNER),
        tok(CONV_CH),
        tok(LANES),
        pl.BlockSpec((1, MLA_HEADS, HEAD_PAD, tm), lambda i, j: (i, 0, 0, j)),
        pl.BlockSpec((1, MLA_HEADS, tm, HEAD_PAD), lambda i, j: (i, 0, j, 0)),
        pl.BlockSpec((1, MLA_HEADS, V_EXT, tm), lambda i, j: (i, 0, 0, j)),
    )
    return pl.pallas_call(
        _mix_in_kernel,
        out_shape=out_shape,
        grid=(b, s // tm),
        in_specs=in_specs,
        out_specs=out_specs,
        scratch_shapes=[pltpu.VMEM((tm + 2 * CONV_HALO, CONV_CH), F32)],
        compiler_params=pltpu.CompilerParams(
            dimension_semantics=("arbitrary", "arbitrary"), vmem_limit_bytes=VMEM_LIMIT),
        name="mix_in",
    )(x, x, x, mod, g, win, cw, cb, dtb, qg, wq, kvg, wkn, wv, cos_t, sin_t, cos_n, sin_n)


def _split3(x):
    hi = x.astype(BF16)
    r1 = x - hi.astype(F32)
    mid = r1.astype(BF16)
    lo = (r1 - mid.astype(F32)).astype(BF16)
    return hi, mid, lo


def _ssd_direction(xbc, dt, a_row, state_ref, y_ref, y_rows, d_row, *, backward):
    L = SSD_CHUNK
    pair_w = 2 * SSD_HEADDIM
    pairs_per_group = SSD_HEADS // SSD_GROUPS // 2
    off = SSD_HEADS if backward else 0
    edge = 0 if backward else L - 1
    row_i = lax.broadcasted_iota(jnp.int32, (L, L), 0)
    col_i = lax.broadcasted_iota(jnp.int32, (L, L), 1)
    keep = (row_i <= col_i) if backward else (row_i >= col_i)
    left = col_i < SSD_HEADDIM
    tri = jnp.where(keep, 1.0, 0.0).astype(BF16)
    a_dt = dt * a_row
    hi, mid, lo = _split3(a_dt)
    a_cs = _dot(tri, hi) + _dot(tri, mid) + _dot(tri, lo)
    a_cs_t = a_cs.T
    src_t = (a_cs - jnp.log2(dt)).T
    yield
    c_off = SSD_INNER + SSD_GROUPS * SSD_STATE
    for grp in range(SSD_GROUPS):
        b_g = xbc[:, SSD_INNER + grp * SSD_STATE:SSD_INNER + (grp + 1) * SSD_STATE]
        c_bf = xbc[:, c_off + grp * SSD_STATE:c_off + (grp + 1) * SSD_STATE].astype(BF16)
        cb = _dot_nt(c_bf, b_g.astype(BF16))
        b_t = b_g.T
        states = [state_ref[grp * pairs_per_group + j] for j in range(pairs_per_group)]
        y_off = _dot(c_bf, jnp.concatenate(states, axis=1).astype(BF16))
        yield
        for j in range(pairs_per_group):
            pr = grp * pairs_per_group + j
            xs_pair = xbc[:, pr * pair_w:(pr + 1) * pair_w]
            rhs = jnp.concatenate([jnp.where(left, xs_pair, 0.0).astype(BF16),
                                   jnp.where(left, 0.0, xs_pair).astype(BF16)], axis=0)
            m_parts, w_parts, col_parts, tot_parts = [], [], [], []
            for hd in (2 * pr, 2 * pr + 1):
                col = jnp.broadcast_to(a_cs[:, off + hd:off + hd + 1], (L, L))
                src = src_t[off + hd:off + hd + 1, :]
                decay_dt = jnp.exp2(jnp.where(keep, col - src, -jnp.inf))
                m_parts.append((cb * decay_dt).astype(BF16))
                a_tot = a_cs_t[off + hd:off + hd + 1, edge:edge + 1]
                w_parts.append((b_t * jnp.exp2(a_tot - src)).astype(BF16))
                col_parts.append(col)
                tot_parts.append(a_tot)
            y = _dot(jnp.concatenate(m_parts, axis=1), rhs)
            upd = _dot(jnp.concatenate(w_parts, axis=1), rhs)
            yield
            y = y + jnp.exp2(jnp.where(left, col_parts[0], col_parts[1])) \
                * y_off[:, j * pair_w:(j + 1) * pair_w]
            if d_row is not None:
                y = y + d_row[:, pr * pair_w:(pr + 1) * pair_w] * xs_pair
            y_ref[0, y_rows, pr * pair_w:(pr + 1) * pair_w] = y.astype(BF16)
            keep_frac = jnp.exp2(jnp.where(left[0:1], tot_parts[0], tot_parts[1]))
            state_ref[pr] = keep_frac * states[j] + upd
            yield


def _ssd_kernel(xf_ref, xb_ref, dtf_ref, dtb_ref, alog_ref, d_ref, yf_ref, yb_ref, sf_ref, sb_ref):
    @pl.when(pl.program_id(1) == 0)
    def _():
        sf_ref[...] = jnp.zeros_like(sf_ref)
        sb_ref[...] = jnp.zeros_like(sb_ref)

    a_row = -jnp.exp(alog_ref[...]) * math.log2(math.e)
    for k in range(SSD_CHUNKS_PER_STEP):
        rf = slice(k * SSD_CHUNK, (k + 1) * SSD_CHUNK)
        rb = slice((SSD_CHUNKS_PER_STEP - 1 - k) * SSD_CHUNK, (SSD_CHUNKS_PER_STEP - k) * SSD_CHUNK)
        fwd = _ssd_direction(xf_ref[0, rf, :], dtf_ref[0, rf, :], a_row, sf_ref, yf_ref, rf,
                             d_ref[...], backward=False)
        bwd = _ssd_direction(xb_ref[0, rb, :], dtb_ref[0, rb, :], a_row, sb_ref, yb_ref, rb,
                             None, backward=True)
        for _ in zip(fwd, bwd):
            pass


def _ssd(xbc, dt, alog_pad, d_exp):
    b, s, _ = xbc.shape
    rows = SSD_CHUNKS_PER_STEP * SSD_CHUNK
    nc = s // rows
    fwd = lambda w: pl.BlockSpec((1, rows, w), lambda i, c: (i, c, 0))
    bwd = lambda w: pl.BlockSpec((1, rows, w), lambda i, c: (i, nc - 1 - c, 0))
    y_shape = jax.ShapeDtypeStruct((b, s, SSD_INNER), BF16)
    state = pltpu.VMEM((SSD_HEADS // 2, SSD_STATE, 2 * SSD_HEADDIM), F32)
    return pl.pallas_call(
        _ssd_kernel,
        out_shape=(y_shape, y_shape),
        grid=(b, nc),
        in_specs=[fwd(CONV_CH), bwd(CONV_CH), fwd(LANES), bwd(LANES),
                  _const_spec((1, LANES)), _const_spec((1, SSD_INNER))],
        out_specs=(fwd(SSD_INNER), bwd(SSD_INNER)),
        scratch_shapes=[state, state],
        compiler_params=pltpu.CompilerParams(
            dimension_semantics=("arbitrary", "arbitrary"), vmem_limit_bytes=VMEM_LIMIT),
        name="ssd",
    )(xbc, xbc, dt, dt, alog_pad, d_exp)


def _fold_rows(t, op):
    r, c = t.shape
    return op(t.reshape(r // SUBLANES, SUBLANES, c), axis=0)


def _attn_kernel(q_ref, k_ref, v_ref, o_ref, sa_scr, sb_scr, ma_scr, mb_scr, acc_scr):
    t = pl.program_id(0)
    q_t = q_ref[0]
    tq = q_t.shape[1]
    n_steps = k_ref.shape[1] // TK_ATTN
    n_sub = TK_ATTN // TK_SUB
    tq_part = tq // n_steps

    def scores(s_new, j, m_run, tk):
        start = pl.multiple_of(j * tk, tk)
        s_t = _dot(k_ref[0, pl.ds(start, tk), :], q_t)
        s_new[pl.ds(start, tk), :] = s_t
        return jnp.maximum(m_run, _fold_rows(s_t, jnp.max))

    def values(s_old, j, m, acc, tk):
        start = pl.multiple_of(j * tk, tk)
        p_t = jnp.exp2(s_old[pl.ds(start, tk), :] - m)
        return acc + _dot(v_ref[0, :, pl.ds(start, tk)], p_t.astype(BF16))

    def finish(i):
        cols = pl.ds(pl.multiple_of(i * tq_part, tq_part), tq_part)
        out = (acc_scr[0:V_DIM, cols] / acc_scr[V_DIM:V_DIM + 1, cols]).T
        o_ref[0, cols, :] = out.astype(BF16)

    m_init = jnp.full((SUBLANES, tq), -jnp.inf, F32)
    acc_init = jnp.zeros((V_EXT, tq), F32)

    @pl.when(t == 0)
    def _():
        ma_scr[...] = lax.fori_loop(
            0, n_steps, lambda i, m_run: scores(sa_scr, i, m_run, TK_ATTN), m_init)
        acc_scr[...] = jnp.ones_like(acc_scr)

    def steady(s_new, m_new, s_old, m_old):
        m = jnp.max(m_old[...], axis=0, keepdims=True)

        def body(i, carry):
            m_run, acc = carry
            for u in range(n_sub):
                acc = values(s_old, i * n_sub + u, m, acc, TK_SUB)
                m_run = scores(s_new, i * n_sub + u, m_run, TK_SUB)
                if u == n_sub // 2:
                    finish(i)
            return m_run, acc

        m_run, acc = lax.fori_loop(0, n_steps, body, (m_init, acc_init))
        m_new[...] = m_run
        acc_scr[...] = acc

    odd = lax.rem(t, 2) == 1
    pl.when(odd)(lambda: steady(sb_scr, mb_scr, sa_scr, ma_scr))
    pl.when(jnp.logical_and(t > 0, jnp.logical_not(odd)))(
        lambda: steady(sa_scr, ma_scr, sb_scr, mb_scr))


def _attention(q_t, k, v_t):
    b, nh, _, s = q_t.shape
    n_q = s // TQ
    n_items = b * nh * n_q
    item = lambda t, lag: jnp.clip(t - lag, 0, n_items - 1)
    return pl.pallas_call(
        _attn_kernel,
        out_shape=jax.ShapeDtypeStruct((b, s, nh * V_DIM), BF16),
        grid=(n_items + 2,),
        in_specs=[
            pl.BlockSpec((1, HEAD_PAD, TQ), lambda t: (item(t, 0) // n_q, 0, item(t, 0) % n_q)),
            pl.BlockSpec((1, s, HEAD_PAD), lambda t: (item(t, 0) // n_q, 0, 0)),
            pl.BlockSpec((1, V_EXT, s), lambda t: (item(t, 1) // n_q, 0, 0)),
        ],
        out_specs=pl.BlockSpec(
            (1, TQ, V_DIM),
            lambda t: (item(t, 2) // (nh * n_q), item(t, 2) % n_q, (item(t, 2) // n_q) % nh)),
        scratch_shapes=[pltpu.VMEM((s, TQ), F32), pltpu.VMEM((s, TQ), F32),
                        pltpu.VMEM((SUBLANES, TQ), F32), pltpu.VMEM((SUBLANES, TQ), F32),
                        pltpu.VMEM((V_EXT, TQ), F32)],
        compiler_params=pltpu.CompilerParams(
            dimension_semantics=("arbitrary",), vmem_limit_bytes=VMEM_LIMIT),
        name="attention",
    )(q_t.reshape(b * nh, HEAD_PAD, s), k.reshape(b * nh, s, HEAD_PAD),
      v_t.reshape(b * nh, V_EXT, s))


def _pad_cols(w, width):
    return jnp.pad(w, ((0, 0), (0, width - w.shape[1])))


def _prep_w_in(w_in):
    cuts = np.cumsum([SSD_INNER, CONV_CH, 2 * SSD_HEADS, Q_LORA, KV_LORA])
    w_z, w_xbc, w_dt, w_q, w_kv, w_kr = jnp.split(w_in, [int(c) for c in cuts], axis=1)
    t1, t2 = w_kr[:, :QK_ROPE // 2], w_kr[:, QK_ROPE // 2:]
    kra = _pad_cols(jnp.concatenate([t1, t2], axis=1), LANES)
    return jnp.concatenate([w_z, w_xbc, _pad_cols(w_dt, LANES), w_q, w_kv, kra],
                           axis=1).astype(BF16)


def _prep_w_q(w_q_b):
    return w_q_b.T.astype(BF16)


def _prep_w_kv(w_kv_b):
    w = w_kv_b.reshape(KV_LORA, MLA_HEADS, QK_NOPE + V_DIM)
    w_kn = w[:, :, :QK_NOPE].reshape(KV_LORA, MLA_HEADS * QK_NOPE).astype(BF16)
    w_v_t = w[:, :, QK_NOPE:].reshape(KV_LORA, MLA_HEADS * V_DIM).T.astype(BF16)
    return w_kn, w_v_t


def kernel(x, c, positions, ada_w, ada_b, norm_g, ffn_w_gate, ffn_w_up, ffn_w_down, w_in, conv_w,
           conv_b, dt_bias, a_log, d_skip, ssd_norm_g, q_norm_g, w_q_b, kv_norm_g, w_kv_b,
           attn_norm_g, w_out, final_norm_g):
    b, s, d = x.shape
    depth = ada_w.shape[0]
    assert s % TM_FFN == 0 and TM_FFN % ROWS_OUT == 0 and s % TM_MIX == 0
    assert s % (SSD_CHUNK * SSD_CHUNKS_PER_STEP) == 0
    assert s % TQ == 0 and s % TK_ATTN == 0 and s % ROPE_TS == 0

    c_pad = jnp.pad(c, ((0, SUBLANES - b), (0, 0)))
    mod_all = _ada(c_pad, ada_w, ada_b)
    tabs = _rope_tables(positions)
    fg = final_norm_g.reshape(1, d)

    for l in range(depth):
        mod = mod_all[l, :b].reshape(b, N_MOD, d)
        bf = lambda w: w.astype(BF16)
        x = _ffn(x, mod, norm_g[l, 0].reshape(1, d), bf(ffn_w_gate[l, 0]), bf(ffn_w_up[l, 0]),
                 bf(ffn_w_down[l, 0]), fg, row0=0, final=False)

        w_kn, w_v_t = _prep_w_kv(w_kv_b[l])
        dtb = _pad_cols(dt_bias[l].reshape(1, 2 * SSD_HEADS), LANES)
        z, xbc, dt, q_t, k, v_t = _mix_in(
            x, mod, norm_g[l, 1].reshape(1, d), _prep_w_in(w_in[l]), conv_w[l],
            conv_b[l].reshape(1, CONV_CH), dtb, q_norm_g[l].reshape(1, Q_LORA), _prep_w_q(w_q_b[l]),
            kv_norm_g[l].reshape(1, KV_LORA), w_kn, w_v_t, tabs)

        alog_pad = _pad_cols(a_log[l].reshape(1, 2 * SSD_HEADS), LANES)
        d_exp = jnp.repeat(d_skip[l], SSD_HEADDIM).reshape(1, SSD_INNER)
        y_f, y_b = _ssd(xbc, dt, alog_pad, d_exp)
        attn = _attention(q_t, k, v_t)

        mixer = (y_f, y_b, z, attn, ssd_norm_g[l].reshape(1, SSD_INNER),
                 attn_norm_g[l].reshape(1, MLA_HEADS * V_DIM),
                 bf(w_out[l, :SSD_INNER]), bf(w_out[l, SSD_INNER:]))
        x = _ffn(x, mod, norm_g[l, 2].reshape(1, d), bf(ffn_w_gate[l, 1]), bf(ffn_w_up[l, 1]),
                 bf(ffn_w_down[l, 1]), fg, row0=6, final=(l == depth - 1), mixer=mixer)
    return x
```
